```python
import jax
import jax.numpy as jnp
from jax import lax
import numpy as np

D_MODEL = 2048
BATCH = 4
SEQ = 2048
DEPTH = 4

HEAD_DIM = 128
NSA_HEADS = 8
NSA_KV_HEADS = 2
NSA_GROUP = NSA_HEADS // NSA_KV_HEADS
CMP_BLOCK = 32
CMP_STRIDE = 16
SEL_BLOCK = 64
N_SEL = 16
WINDOW = 512
SEL_QUERY_BLOCK = 64
FORCE_BONUS = 1e4
DSA_HEADS = 8
KV_LATENT = 256
IDX_HEADS = 4
IDX_DIM = 64
DSA_TOPK_MAX = 256
QUERY_BLOCK = 128
GMLP_GROUPS = 8
GMLP_GROUP_DIM = 128
GMLP_WIDTH = GMLP_GROUPS * GMLP_GROUP_DIM
GMLP_CHUNK = 128
N_BRANCHES = 3
BRANCH_WIDTH = D_MODEL // 2
KV_COLS = NSA_KV_HEADS * HEAD_DIM
IN_SPLITS = (NSA_HEADS * HEAD_DIM,
             KV_COLS, KV_COLS,
             KV_COLS, KV_COLS,
             KV_COLS, KV_COLS,
             NSA_HEADS * 3,
             DSA_HEADS * HEAD_DIM,
             KV_LATENT,
             IDX_HEADS * IDX_DIM,
             IDX_DIM,
             IDX_HEADS,
             2 * GMLP_WIDTH,
             N_BRANCHES * D_MODEL)
IN_COLS = sum(IN_SPLITS)
N_GROUPS = 4
EXPERTS_PER_GROUP = 4
N_EXPERTS = N_GROUPS * EXPERTS_PER_GROUP
EXPERT_TOPK = 2
D_EXPERT = 512
DEEPNORM_ALPHA = (2 * DEPTH) ** 0.25
DEEPNORM_BETA = (8 * DEPTH) ** -0.25
LN_EPS = 1e-5
NEG_BIG = -1e30

kernel_name = 'hybrid_nsa_dsa_gmlp_hmoe_deepnorm'


def _split_cols(h, sizes):
    out, start = [], 0
    for n in sizes:
        out.append(h[..., start:start + n])
        start += n
    return out


def _layer_norm(x, g, b):
    xf = x.astype(jnp.float32)
    mu = jnp.mean(xf, axis=-1, keepdims=True)
    var = jnp.mean(jnp.square(xf - mu), axis=-1, keepdims=True)
    return ((xf - mu) * lax.rsqrt(var + LN_EPS) * g + b).astype(x.dtype)


def _rms_norm(x, g):
    xf = x.astype(jnp.float32)
    return (xf * lax.rsqrt(jnp.mean(xf * xf, axis=-1, keepdims=True) + LN_EPS) * g).astype(x.dtype)


def _masked_softmax(s, valid):
    s = jnp.where(valid, s.astype(jnp.float32), NEG_BIG)
    p = jax.nn.softmax(s, axis=-1)
    return jnp.where(valid, p, 0.0)


def _alibi_slopes(n_heads):
    return 2.0 ** (-8.0 * jnp.arange(1, n_heads + 1, dtype=jnp.float32) / n_heads)


def _nsa_mixer(q, k_cmp, v_cmp, k_slc, v_slc, k_win, v_win, gates, cmp_w1, cmp_w2, cmp_pe, slopes):
    B, S = q.shape[:2]
    scale = HEAD_DIM ** -0.5
    slopes_g = slopes.reshape(NSA_KV_HEADS, NSA_GROUP)
    t_pos = jnp.arange(S)

    n_cmp = (S - CMP_BLOCK) // CMP_STRIDE + 1
    blk_idx = jnp.arange(n_cmp)[:, None] * CMP_STRIDE + jnp.arange(CMP_BLOCK)[None, :]

    def compress(x, w1, w2, pe):
        xb = x[:, blk_idx] + pe[None, None, :, None, :]
        xb = jnp.moveaxis(xb, 3, 2).reshape(B, n_cmp, NSA_KV_HEADS, CMP_BLOCK * HEAD_DIM)
        return jax.nn.gelu(xb @ w1) @ w2

    kc = compress(k_cmp, cmp_w1[0], cmp_w2[0], cmp_pe[0])
    vc = compress(v_cmp, cmp_w1[1], cmp_w2[1], cmp_pe[1])
    cmp_start = jnp.arange(n_cmp) * CMP_STRIDE
    dist_c = (t_pos[:, None] - (cmp_start + CMP_BLOCK - 1)[None, :]).astype(jnp.float32)
    s_c = jnp.einsum('bthgd,bihd->bhgti', q, kc) * scale - slopes_g[:, :, None, None] * dist_c
    p_c = _masked_softmax(s_c, dist_c >= 0)
    o_cmp = jnp.einsum('bhgti,bihd->bthgd', p_c.astype(q.dtype), vc)

    n_slc = S // SEL_BLOCK
    slc_start = jnp.arange(n_slc) * SEL_BLOCK
    overlap = ((cmp_start[:, None] < slc_start[None, :] + SEL_BLOCK)
               & (cmp_start[:, None] + CMP_BLOCK > slc_start[None, :])).astype(jnp.float32)
    imp = jnp.einsum('bhgti,ij->bhtj', p_c, overlap)
    cur = t_pos // SEL_BLOCK
    j = jnp.arange(n_slc)[None, :]
    blk_valid = j <= cur[:, None]
    forced = (j == 0) | (j == cur[:, None]) | (j == cur[:, None] - 1)
    imp = jnp.where(blk_valid, imp + jnp.where(forced, FORCE_BONUS, 0.0), NEG_BIG)
    n_sel = min(N_SEL, n_slc)
    _, sel = lax.top_k(imp, n_sel)

    ks = k_slc.reshape(B, n_slc, SEL_BLOCK, NSA_KV_HEADS, HEAD_DIM).transpose(0, 3, 1, 2, 4)
    vs = v_slc.reshape(B, n_slc, SEL_BLOCK, NSA_KV_HEADS, HEAD_DIM).transpose(0, 3, 1, 2, 4)
    qb_len = SEL_QUERY_BLOCK
    nqb = S // qb_len
    q_blocks = q.reshape(B, nqb, qb_len, NSA_KV_HEADS, NSA_GROUP, HEAD_DIM).swapaxes(0, 1)
    sel_blocks = sel.reshape(B, NSA_KV_HEADS, nqb, qb_len, n_sel).transpose(2, 0, 1, 3, 4)
    b_ix = jnp.arange(B)[:, None, None, None]
    h_ix = jnp.arange(NSA_KV_HEADS)[None, :, None, None]
    n_keys = n_sel * SEL_BLOCK

    def sel_attend(args):
        qb, sb, start = args
        kg = ks[b_ix, h_ix, sb]
        vg = vs[b_ix, h_ix, sb]
        t = start + jnp.arange(qb_len)
        s_pos = sb[..., None] * SEL_BLOCK + jnp.arange(SEL_BLOCK)
        dist = (t[None, None, :, None, None] - s_pos).astype(jnp.float32)
        s = (jnp.einsum('bqhgd,bhqnld->bhgqnl', qb, kg) * scale
             - slopes_g[None, :, :, None, None, None] * dist[:, :, None])
        p = _masked_softmax(s.reshape(B, NSA_KV_HEADS, NSA_GROUP, qb_len, n_keys),
                            (dist >= 0).reshape(B, NSA_KV_HEADS, 1, qb_len, n_keys))
        return jnp.einsum('bhgqk,bhqkd->bqhgd', p.astype(qb.dtype),
                          vg.reshape(B, NSA_KV_HEADS, qb_len, n_keys, HEAD_DIM))

    o_slc = lax.map(sel_attend, (q_blocks, sel_blocks, jnp.arange(nqb) * qb_len))
    o_slc = o_slc.swapaxes(0, 1).reshape(B, S, NSA_KV_HEADS, NSA_GROUP, HEAD_DIM)

    nwb = S // QUERY_BLOCK
    span = WINDOW + QUERY_BLOCK
    kp = jnp.pad(k_win, ((0, 0), (WINDOW, 0), (0, 0), (0, 0)))
    vp = jnp.pad(v_win, ((0, 0), (WINDOW, 0), (0, 0), (0, 0)))
    win_idx = jnp.arange(nwb)[:, None] * QUERY_BLOCK + jnp.arange(span)[None, :]
    kw = kp[:, win_idx]
    vw = vp[:, win_idx]
    t_w = jnp.arange(nwb)[:, None] * QUERY_BLOCK + jnp.arange(QUERY_BLOCK)[None, :]
    s_w = win_idx - WINDOW
    dist_w = t_w[:, :, None] - s_w[:, None, :]
    valid_w = (dist_w >= 0) & (dist_w < WINDOW) & (s_w[:, None, :] >= 0)
    qw = q.reshape(B, nwb, QUERY_BLOCK, NSA_KV_HEADS, NSA_GROUP, HEAD_DIM)
    s = (jnp.einsum('bnqhgd,bnkhd->bhgnqk', qw, kw) * scale
         - slopes_g[:, :, None, None, None] * dist_w.astype(jnp.float32))
    p = _masked_softmax(s, valid_w)
    o_win = jnp.einsum('bhgnqk,bnkhd->bnqhgd', p.astype(q.dtype), vw)
    o_win = o_win.reshape(B, S, NSA_KV_HEADS, NSA_GROUP, HEAD_DIM)

    g = jax.nn.sigmoid(gates.astype(jnp.float32)).astype(q.dtype).reshape(B, S, NSA_KV_HEADS, NSA_GROUP, 3)
    o = g[..., 0:1] * o_cmp + g[..., 1:2] * o_slc + g[..., 2:3] * o_win
    return o.reshape(B, S, NSA_HEADS * HEAD_DIM)


def _dsa_mixer(q, c_kv, q_idx, k_idx, w_idx, w_uk, w_uv, kv_norm_g, slopes):
    B, S = q.shape[:2]
    c = _rms_norm(c_kv, kv_norm_g)
    q_lat = jnp.einsum('bthd,hcd->bthc', q, w_uk) * (HEAD_DIM ** -0.5)
    topk = min(DSA_TOPK_MAX, S // 4)
    nqb = S // QUERY_BLOCK
    s_pos = jnp.arange(S)

    def blk(a):
        return a.reshape((B, nqb, QUERY_BLOCK) + a.shape[2:]).swapaxes(0, 1)

    def attend(args):
        ql, qi, wi, start = args
        t = start + jnp.arange(QUERY_BLOCK)
        rel = jax.nn.relu(jnp.einsum('bqhd,bsd->bqhs', qi, k_idx))
        score = jnp.einsum('bqhs,bqh->bqs', rel, wi).astype(jnp.float32)
        score = jnp.where(s_pos[None, None, :] <= t[None, :, None], score, -jnp.inf)
        _, sel = lax.top_k(score, topk)
        cg = jax.vmap(lambda cb, ib: cb[ib])(c, sel)
        dist = (t[None, :, None] - sel).astype(jnp.float32)
        logits = (jnp.einsum('bqhc,bqkc->bhqk', ql, cg)
                  - slopes[None, :, None, None] * dist[:, None])
        p = _masked_softmax(logits, (dist >= 0)[:, None])
        return jnp.einsum('bhqk,bqkc->bqhc', p.astype(cg.dtype), cg)

    o_lat = lax.map(attend, (blk(q_lat), blk(q_idx), blk(w_idx), jnp.arange(nqb) * QUERY_BLOCK))
    o_lat = o_lat.swapaxes(0, 1).reshape(B, S, DSA_HEADS, KV_LATENT)
    o = jnp.einsum('bthc,hcd->bthd', o_lat, w_uv)
    return o.reshape(B, S, DSA_HEADS * HEAD_DIM)


def _gmlp_mixer(uv, ln_g, ln_b, w_s, b_s):
    B, S = uv.shape[:2]
    z = jax.nn.gelu(uv)
    u, v = z[..., :GMLP_WIDTH], z[..., GMLP_WIDTH:]
    v = _layer_norm(v, ln_g, ln_b)
    nc = S // GMLP_CHUNK
    v = v.reshape(B, nc, GMLP_CHUNK, GMLP_GROUPS, GMLP_GROUP_DIM)
    causal = jnp.tril(jnp.ones((GMLP_CHUNK, GMLP_CHUNK), dtype=bool))
    w = jnp.where(causal[None], w_s, 0.0).astype(v.dtype)
    mixed = jnp.einsum('gts,bcsgd->bctgd', w, v) + b_s.T[None, None, :, :, None]
    return u * mixed.reshape(B, S, GMLP_WIDTH)


def _hier_moe(x, wg_r, bg_r, we_r, be_r, w_gate, w_up, w_down):
    B, S, D = x.shape
    xt = x.reshape(B * S, D)
    n_tok = B * S
    group_prob = jax.nn.softmax((xt @ wg_r + bg_r).astype(jnp.float32), axis=-1)
    gp, gi = lax.top_k(group_prob, 1)
    exp_logits = (xt @ we_r + be_r).astype(jnp.float32).reshape(n_tok, N_GROUPS, EXPERTS_PER_GROUP)
    in_group = exp_logits[jnp.arange(n_tok), gi[:, 0]]
    top_logits, ei = lax.top_k(in_group, EXPERT_TOPK)
    weights = gp * jax.nn.softmax(top_logits, axis=-1)
    expert_id = gi * EXPERTS_PER_GROUP + ei
    combine = jnp.sum(jax.nn.one_hot(expert_id, N_EXPERTS, dtype=jnp.float32) * weights[..., None], axis=1)
    h = jax.nn.silu(jnp.einsum('td,edf->etf', xt, w_gate)) * jnp.einsum('td,edf->etf', xt, w_up)
    h = h * combine.T[:, :, None].astype(h.dtype)
    return jnp.einsum('etf,efd->td', h, w_down).reshape(B, S, D)


def setup_inputs(seed: int = 0) -> dict:
    key = jax.random.key(seed)
    ks = jax.random.split(key, 25)

    def nrm(k, shape, scale):
        return jax.random.normal(k, shape, jnp.float32) * scale

    L, D, W = DEPTH, D_MODEL, BRANCH_WIDTH
    return {
        'x': nrm(ks[0], (BATCH, SEQ, D), 1.0),
        'w_in': nrm(ks[1], (L, D, IN_COLS), D ** -0.5),
        'cmp_w1': nrm(ks[2], (L, 2, CMP_BLOCK * HEAD_DIM, HEAD_DIM), (CMP_BLOCK * HEAD_DIM) ** -0.5),
        'cmp_w2': nrm(ks[3], (L, 2, HEAD_DIM, HEAD_DIM), HEAD_DIM ** -0.5),
        'cmp_pe': nrm(ks[4], (L, 2, CMP_BLOCK, HEAD_DIM), 0.1),
        'w_uk': nrm(ks[5], (L, DSA_HEADS, KV_LATENT, HEAD_DIM), KV_LATENT ** -0.5),
        'w_uv': nrm(ks[6], (L, DSA_HEADS, KV_LATENT, HEAD_DIM), KV_LATENT ** -0.5),
        'kv_norm_g': 1.0 + nrm(ks[7], (L, KV_LATENT), 0.01),
        'gmlp_ln_g': 1.0 + nrm(ks[8], (L, GMLP_WIDTH), 0.01),
        'gmlp_ln_b': nrm(ks[9], (L, GMLP_WIDTH), 0.01),
        'gmlp_w_s': nrm(ks[10], (L, GMLP_GROUPS, GMLP_CHUNK, GMLP_CHUNK), GMLP_CHUNK ** -0.5),
        'gmlp_b_s': 1.0 + nrm(ks[11], (L, GMLP_GROUPS, GMLP_CHUNK), 0.1),
        'w_branch': nrm(ks[12], (L, N_BRANCHES, W, D), W ** -0.5 * DEEPNORM_BETA),
        'w_out': nrm(ks[13], (L, D, D), D ** -0.5 * DEEPNORM_BETA),
        'ln1_g': 1.0 + nrm(ks[14], (L, D), 0.01),
        'ln1_b': nrm(ks[15], (L, D), 0.01),
        'router_group_w': nrm(ks[16], (L, D, N_GROUPS), D ** -0.5),
        'router_group_b': nrm(ks[17], (L, N_GROUPS), 0.01),
        'router_expert_w': nrm(ks[18], (L, D, N_EXPERTS), D ** -0.5),
        'router_expert_b': nrm(ks[19], (L, N_EXPERTS), 0.01),
        'expert_w_gate': nrm(ks[20], (L, N_EXPERTS, D, D_EXPERT), D ** -0.5),
        'expert_w_up': nrm(ks[21], (L, N_EXPERTS, D, D_EXPERT), D ** -0.5 * DEEPNORM_BETA),
        'expert_w_down': nrm(ks[22], (L, N_EXPERTS, D_EXPERT, D), D_EXPERT ** -0.5 * DEEPNORM_BETA),
        'ln2_g': 1.0 + nrm(ks[23], (L, D), 0.01),
        'ln2_b': nrm(ks[24], (L, D), 0.01),
    }


def reference(x, w_in, cmp_w1, cmp_w2, cmp_pe, w_uk, w_uv, kv_norm_g, gmlp_ln_g, gmlp_ln_b,
              gmlp_w_s, gmlp_b_s, w_branch, w_out, ln1_g, ln1_b, router_group_w, router_group_b,
              router_expert_w, router_expert_b, expert_w_gate, expert_w_up, expert_w_down,
              ln2_g, ln2_b):
    B, S, D = x.shape
    slopes = _alibi_slopes(NSA_HEADS)
    for l in range(DEPTH):
        h = x @ w_in[l]
        (q_a, k_c, v_c, k_s, v_s, k_w, v_w, g_nsa, q_b, c_kv, q_idx, k_idx, w_idx,
         uv, g_merge) = _split_cols(h, IN_SPLITS)
        kv_shape = (B, S, NSA_KV_HEADS, HEAD_DIM)
        o_a = _nsa_mixer(q_a.reshape(B, S, NSA_KV_HEADS, NSA_GROUP, HEAD_DIM),
                         k_c.reshape(kv_shape), v_c.reshape(kv_shape),
                         k_s.reshape(kv_shape), v_s.reshape(kv_shape),
                         k_w.reshape(kv_shape), v_w.reshape(kv_shape),
                         g_nsa, cmp_w1[l], cmp_w2[l], cmp_pe[l], slopes)
        o_b = _dsa_mixer(q_b.reshape(B, S, DSA_HEADS, HEAD_DIM), c_kv,
                         q_idx.reshape(B, S, IDX_HEADS, IDX_DIM), k_idx, w_idx,
                         w_uk[l], w_uv[l], kv_norm_g[l], slopes)
        o_c = _gmlp_mixer(uv, gmlp_ln_g[l], gmlp_ln_b[l], gmlp_w_s[l], gmlp_b_s[l])
        branches = jnp.stack([o_a, o_b, o_c], axis=2)
        y = jnp.einsum('bsnw,nwd->bsnd', branches, w_branch[l])
        gates = jax.nn.sigmoid(g_merge.astype(jnp.float32)).astype(x.dtype).reshape(B, S, N_BRANCHES, D)
        mixed = jnp.sum(gates * y, axis=2) @ w_out[l]
        x = _layer_norm(DEEPNORM_ALPHA * x + mixed, ln1_g[l], ln1_b[l])
        ffn = _hier_moe(x, router_group_w[l], router_group_b[l], router_expert_w[l], router_expert_b[l],
                        expert_w_gate[l], expert_w_up[l], expert_w_down[l])
        x = _layer_norm(DEEPNORM_ALPHA * x + ffn, ln2_g[l], ln2_b[l])
    return x
```

```python
import functools

import jax
import jax.numpy as jnp
from jax import lax
from jax.experimental import pallas as pl
from jax.experimental.pallas import tpu as pltpu

HEAD_DIM = 128
NSA_HEADS = 8
NSA_KV_HEADS = 2
NSA_GROUP = NSA_HEADS // NSA_KV_HEADS
CMP_BLOCK = 32
CMP_STRIDE = 16
SEL_BLOCK = 64
N_SEL = 16
WINDOW = 512
FORCE_BONUS = 1e4
DSA_HEADS = 8
KV_LATENT = 256
IDX_HEADS = 4
IDX_DIM = 64
DSA_TOPK_MAX = 256
QUERY_BLOCK = 128
GMLP_GROUPS = 8
GMLP_GROUP_DIM = 128
GMLP_WIDTH = GMLP_GROUPS * GMLP_GROUP_DIM
GMLP_CHUNK = 128
N_BRANCHES = 3
N_GROUPS = 4
EXPERTS_PER_GROUP = 4
N_EXPERTS = N_GROUPS * EXPERTS_PER_GROUP
D_EXPERT = 512
LN_EPS = 1e-5
NEG_BIG = -1e30
ATTN_SCALE = HEAD_DIM ** -0.5

LANES = 128
INT_MIN = -2 ** 31
MIB = 1024 * 1024

H1_Q_A = 0
H1_KS, H1_VS, H1_KW, H1_VW = 1024, 1280, 1536, 1792
H1_Q_B = 2048
H1_Q_IDX = 3072
H1_K_IDX = 3584
H1_COLS = 3840
H2_UV = 0
H2_KC, H2_VC = 2048, 2304
H2_CKV = 2560
H2_GN = 2816
H2_WIDX = 3072
H2_COLS = 3200

MM_TM = 1024
CMP_TQ = 256
SEL_TK = 256
ATT_TQ = 128
WIN_TK = 128
DSA_TK = 512
MERGE_TM = 512
MERGE_TN = 512
OUT_TM = 256
MOE_TM = 512

_F32 = jnp.float32
_BF16 = jnp.bfloat16


def _cparams(n_axes, vmem_mib):
    return pltpu.CompilerParams(dimension_semantics=("arbitrary",) * n_axes,
                                vmem_limit_bytes=vmem_mib * MIB)


def _dot(a, b):
    return jnp.dot(a, b, preferred_element_type=_F32)


def _dot_nt(a, b):
    return lax.dot_general(a, b, (((1,), (1,)), ((), ())), preferred_element_type=_F32)


def _mm_kernel(a_ref, b_ref, o_ref, a_bf_ref, *, act):
    @pl.when(pl.program_id(1) == 0)
    def _():
        a_bf_ref[...] = a_ref[...].astype(_BF16)

    acc = _dot(a_bf_ref[...], b_ref[...])
    if act == "sigmoid":
        acc = jax.nn.sigmoid(acc)
    o_ref[...] = acc.astype(o_ref.dtype)


def _matmul(a, b, out_dtype, tn, act=None):
    m, k = a.shape
    n = b.shape[1]
    tm = min(MM_TM, m)
    return pl.pallas_call(
        functools.partial(_mm_kernel, act=act),
        grid=(m // tm, n // tn),
        in_specs=[pl.BlockSpec((tm, k), lambda i, j: (i, 0)),
                  pl.BlockSpec((k, tn), lambda i, j: (0, j))],
        out_specs=pl.BlockSpec((tm, tn), lambda i, j: (i, j)),
        out_shape=jax.ShapeDtypeStruct((m, n), out_dtype),
        scratch_shapes=[pltpu.VMEM((tm, k), _BF16)],
        compiler_params=_cparams(2, 48),
        name="proj_matmul",
    )(a, b)


def _compress_kernel(x_ref, pe_ref, w1_ref, w2_ref, o_ref):
    x = x_ref[0, 0, 0]
    half = CMP_STRIDE * HEAD_DIM
    top = _dot((x + pe_ref[0, 0:1, :]).astype(_BF16), w1_ref[0, :half, :])
    bot = _dot((x + pe_ref[0, 1:2, :]).astype(_BF16), w1_ref[0, half:, :])
    nc = x.shape[0]
    pre = top + pltpu.roll(bot, nc - 1, 0)
    o_ref[0, 0, 0] = _dot(jax.nn.gelu(pre).astype(_BF16), w2_ref[0]).astype(o_ref.dtype)


def _compress(xc, pe, w1, w2):
    b, _, hkv, nc, width = xc.shape
    return pl.pallas_call(
        _compress_kernel,
        grid=(b, 2, hkv),
        in_specs=[pl.BlockSpec((1, 1, 1, nc, width), lambda i, j, h: (i, j, h, 0, 0)),
                  pl.BlockSpec((1, 2, width), lambda i, j, h: (j, 0, 0)),
                  pl.BlockSpec((1, 2 * width, HEAD_DIM), lambda i, j, h: (j, 0, 0)),
                  pl.BlockSpec((1, HEAD_DIM, HEAD_DIM), lambda i, j, h: (j, 0, 0))],
        out_specs=pl.BlockSpec((1, 1, 1, nc, HEAD_DIM), lambda i, j, h: (i, j, h, 0, 0)),
        out_shape=jax.ShapeDtypeStruct((b, 2, hkv, nc, HEAD_DIM), _BF16),
        compiler_params=_cparams(3, 32),
        name="nsa_compress",
    )(xc, pe, w1, w2)


def _cmp_select_kernel(slopes_ref, q_ref, kc_ref, vc_ref, gn_ref, ocg_ref, mask_ref, *, seq):
    h = pl.program_id(1)
    qt = pl.program_id(2)
    tq = q_ref.shape[1]
    nc = kc_ref.shape[3]
    n_cmp = (seq - CMP_BLOCK) // CMP_STRIDE + 1
    n_slc = seq // SEL_BLOCK

    t_col = qt * tq + lax.broadcasted_iota(jnp.int32, (tq, 1), 0)
    i_row = lax.broadcasted_iota(jnp.int32, (1, nc), 1)
    dist_i = t_col - (i_row * CMP_STRIDE + (CMP_BLOCK - 1))
    valid = (dist_i >= 0) & (i_row < n_cmp)
    dist = dist_i.astype(_F32)

    kc = kc_ref[0, 0, 0]
    vc = vc_ref[0, 0, 0]
    gn = jax.nn.sigmoid(gn_ref[0])
    psum = jnp.zeros((tq, nc), _F32)
    for g in range(NSA_GROUP):
        qg = q_ref[0, :, g * HEAD_DIM:(g + 1) * HEAD_DIM]
        slope = slopes_ref[h * NSA_GROUP + g]
        s = _dot_nt(qg, kc) * ATTN_SCALE - slope * dist
        s = jnp.where(valid, s, NEG_BIG)
        e = jnp.exp(s - jnp.max(s, axis=1, keepdims=True))
        p = e / jnp.sum(e, axis=1, keepdims=True)
        p = jnp.where(valid, p, 0.0)
        o = _dot(p.astype(_BF16), vc)
        ocg_ref[0, :, g * HEAD_DIM:(g + 1) * HEAD_DIM] = gn[:, 3 * g:3 * g + 1] * o
        psum = psum + p

    ci = lax.broadcasted_iota(jnp.int32, (nc, LANES), 0) * CMP_STRIDE
    sj = lax.broadcasted_iota(jnp.int32, (nc, LANES), 1) * SEL_BLOCK
    overlap = ((ci < sj + SEL_BLOCK) & (ci + CMP_BLOCK > sj)).astype(_F32)
    imp = jnp.dot(psum, overlap, preferred_element_type=_F32, precision=lax.Precision.HIGHEST)

    j_row = lax.broadcasted_iota(jnp.int32, (1, LANES), 1)
    cur = t_col >> (SEL_BLOCK.bit_length() - 1)
    forced = (j_row == 0) | (j_row == cur) | (j_row == cur - 1)
    imp = jnp.where(j_row <= cur, imp + jnp.where(forced, FORCE_BONUS, 0.0), NEG_BIG)
    imp = jnp.where(j_row < n_slc, imp, -3e38)
    rank = jnp.zeros((tq, LANES), _F32)
    for jp in range(n_slc):
        col = imp[:, jp:jp + 1]
        before = (col > imp) | ((col == imp) & (j_row > jp))
        rank = rank + before.astype(_F32)
    sel = (rank < min(N_SEL, n_slc)).astype(_BF16)

    tk = mask_ref.shape[4]
    bj = lax.broadcasted_iota(jnp.int32, (LANES, tk), 0)
    ks = lax.broadcasted_iota(jnp.int32, (LANES, tk), 1)
    for kt in range(mask_ref.shape[2]):
        expand = (bj == ((ks + kt * tk) >> (SEL_BLOCK.bit_length() - 1))).astype(_BF16)
        mask_ref[0, 0, kt] = _dot(sel, expand).astype(mask_ref.dtype)


def _cmp_select(slopes, h1, kcv, h2, seq):
    b = h1.shape[0]
    nc = kcv.shape[3]
    tq = min(CMP_TQ, seq)
    tk = min(SEL_TK, seq)
    nkt = seq // tk
    qw = NSA_GROUP * HEAD_DIM
    return pl.pallas_call(
        functools.partial(_cmp_select_kernel, seq=seq),
        grid=(b, NSA_KV_HEADS, seq // tq),
        in_specs=[pl.BlockSpec(memory_space=pltpu.SMEM),
                  pl.BlockSpec((1, tq, qw), lambda i, h, t: (i, t, h)),
                  pl.BlockSpec((1, 1, 1, nc, HEAD_DIM), lambda i, h, t: (i, 0, h, 0, 0)),
                  pl.BlockSpec((1, 1, 1, nc, HEAD_DIM), lambda i, h, t: (i, 1, h, 0, 0)),
                  pl.BlockSpec((1, tq, LANES), lambda i, h, t: (i, t, H2_GN // LANES + h))],
        out_specs=[pl.BlockSpec((1, tq, qw), lambda i, h, t: (i, t, h)),
                   pl.BlockSpec((1, 1, nkt, tq, tk), lambda i, h, t: (i, h, 0, t, 0))],
        out_shape=[jax.ShapeDtypeStruct((b, seq, NSA_HEADS * HEAD_DIM), _F32),
                   jax.ShapeDtypeStruct((b, NSA_KV_HEADS, nkt, seq, tk), _BF16)],
        compiler_params=_cparams(3, 32),
        name="nsa_cmp_select",
    )(slopes, h1, kcv, kcv, h2)


def _flash(q4, k_ref, v_ref, kt_lo, kt_hi, tk, t4, slope4, mask_fn):
    rows = q4.shape[0]

    def body(kt, carry):
        m, l, acc = carry
        start = pl.multiple_of(kt * tk, tk)
        k = k_ref[0, pl.ds(start, tk), :]
        v = v_ref[0, pl.ds(start, tk), :]
        s_pos = start + lax.broadcasted_iota(jnp.int32, (1, tk), 1)
        dist_i = t4 - s_pos
        valid = mask_fn(kt, dist_i)
        s = _dot_nt(q4, k) * ATTN_SCALE - slope4 * dist_i.astype(_F32)
        s = jnp.where(valid, s, NEG_BIG)
        m_new = jnp.maximum(m, jnp.max(s, axis=1, keepdims=True))
        p = jnp.where(valid, jnp.exp(s - m_new), 0.0)
        alpha = jnp.exp(m - m_new)
        l = alpha * l + jnp.sum(p, axis=1, keepdims=True)
        acc = alpha * acc + _dot(p.astype(_BF16), v)
        return m_new, l, acc

    init = (jnp.full((rows, 1), NEG_BIG, _F32), jnp.zeros((rows, 1), _F32),
            jnp.zeros((rows, HEAD_DIM), _F32))
    _, l, acc = lax.fori_loop(kt_lo, kt_hi + 1, body, init)
    return acc * jnp.where(l > 0, 1.0 / l, 0.0)


def _sel_win_kernel(slopes_ref, q_ref, ks_ref, vs_ref, kw_ref, vw_ref, mask_ref, ocg_ref, gn_ref,
                    o_ref):
    h = pl.program_id(1)
    qt = pl.program_id(2)
    tq = q_ref.shape[1]
    rows = NSA_GROUP * tq
    q0 = qt * tq
    q4 = jnp.concatenate([q_ref[0, :, g * HEAD_DIM:(g + 1) * HEAD_DIM] for g in range(NSA_GROUP)],
                         axis=0)
    row = lax.broadcasted_iota(jnp.int32, (rows, 1), 0)
    t4 = q0 + (row & (tq - 1))
    g4 = row >> (tq.bit_length() - 1)
    slope4 = jnp.zeros((rows, 1), _F32)
    for g in range(NSA_GROUP):
        slope4 = jnp.where(g4 == g, slopes_ref[h * NSA_GROUP + g], slope4)

    sel_tk = mask_ref.shape[4]

    def sel_mask(kt, dist_i):
        mk = mask_ref[0, 0, kt].astype(_F32)
        mk4 = jnp.concatenate([mk] * NSA_GROUP, axis=0)
        return (mk4 > 0.5) & (dist_i >= 0)

    def win_mask(kt, dist_i):
        return (dist_i >= 0) & (dist_i < WINDOW)

    o_slc = _flash(q4, ks_ref, vs_ref, 0, (q0 + tq - 1) // sel_tk, sel_tk, t4, slope4, sel_mask)
    win_tk = min(WIN_TK, ks_ref.shape[1])
    win_lo = jnp.maximum(q0 - (WINDOW - 1), 0) // win_tk
    o_win = _flash(q4, kw_ref, vw_ref, win_lo, (q0 + tq - 1) // win_tk, win_tk, t4, slope4, win_mask)

    gn = jax.nn.sigmoid(gn_ref[0])
    for g in range(NSA_GROUP):
        o = (ocg_ref[0, :, g * HEAD_DIM:(g + 1) * HEAD_DIM]
             + gn[:, 3 * g + 1:3 * g + 2] * o_slc[g * tq:(g + 1) * tq]
             + gn[:, 3 * g + 2:3 * g + 3] * o_win[g * tq:(g + 1) * tq])
        o_ref[0, :, g * HEAD_DIM:(g + 1) * HEAD_DIM] = o.astype(o_ref.dtype)


def _sel_win(slopes, h1, selmask, ocg, h2, seq):
    b = h1.shape[0]
    tq = min(ATT_TQ, seq)
    nkt, tk = selmask.shape[2], selmask.shape[4]
    qw = NSA_GROUP * HEAD_DIM

    def kv_spec(col):
        return pl.BlockSpec((1, seq, HEAD_DIM), lambda i, h, t: (i, 0, col // HEAD_DIM + h))

    return pl.pallas_call(
        _sel_win_kernel,
        grid=(b, NSA_KV_HEADS, seq // tq),
        in_specs=[pl.BlockSpec(memory_space=pltpu.SMEM),
                  pl.BlockSpec((1, tq, qw), lambda i, h, t: (i, t, h)),
                  kv_spec(H1_KS), kv_spec(H1_VS), kv_spec(H1_KW), kv_spec(H1_VW),
                  pl.BlockSpec((1, 1, nkt, tq, tk), lambda i, h, t: (i, h, 0, t, 0)),
                  pl.BlockSpec((1, tq, qw), lambda i, h, t: (i, t, h)),
                  pl.BlockSpec((1, tq, LANES), lambda i, h, t: (i, t, H2_GN // LANES + h))],
        out_specs=pl.BlockSpec((1, tq, qw), lambda i, h, t: (i, t, h)),
        out_shape=jax.ShapeDtypeStruct((b, seq, NSA_HEADS * HEAD_DIM), _BF16),
        compiler_params=_cparams(3, 32),
        name="nsa_sel_win",
    )(slopes, h1, h1, h1, h1, h1, selmask, ocg, h2)


def _rms_kernel(c_ref, g_ref, o_ref):
    c = c_ref[0]
    o_ref[0] = (c * lax.rsqrt(jnp.mean(c * c, axis=-1, keepdims=True) + LN_EPS) * g_ref[...]
                ).astype(o_ref.dtype)


def _rms_norm(h2, g, seq):
    b = h2.shape[0]
    ts = min(512, seq)
    return pl.pallas_call(
        _rms_kernel,
        grid=(b, seq // ts),
        in_specs=[pl.BlockSpec((1, ts, KV_LATENT), lambda i, t: (i, t, H2_CKV // KV_LATENT)),
                  pl.BlockSpec((1, KV_LATENT), lambda i, t: (0, 0))],
        out_specs=pl.BlockSpec((1, ts, KV_LATENT), lambda i, t: (i, t, 0)),
        out_shape=jax.ShapeDtypeStruct((b, seq, KV_LATENT), _BF16),
        compiler_params=_cparams(2, 32),
        name="dsa_rms_norm",
    )(h2, g)


def _dsa_kernel(slopes_ref, qb_ref, qi_ref, ki_ref, wi_ref, cn_ref, wuk_ref, wuv_ref, o_ref,
                key_ref, mask_ref, qlat_ref, m_ref, l_ref, acc_ref, *, topk):
    qt = pl.program_id(1)
    tq = qb_ref.shape[1]
    seq = ki_ref.shape[1]
    tk = mask_ref.shape[2]
    q0 = qt * tq
    rows = DSA_HEADS * tq

    t_col = q0 + lax.broadcasted_iota(jnp.int32, (tq, 1), 0)
    s_row = lax.broadcasted_iota(jnp.int32, (1, seq), 1)
    causal = s_row <= t_col
    ki = ki_ref[0]
    wi = wi_ref[0]
    score = jnp.zeros((tq, seq), _F32)
    for hh in range(IDX_HEADS):
        rel = jnp.maximum(_dot_nt(qi_ref[0, :, hh * LANES:(hh + 1) * LANES], ki), 0.0)
        score = score + wi[:, hh:hh + 1] * rel
    score = jnp.where(score == 0.0, 0.0, score)
    bits = lax.bitcast_convert_type(score, jnp.int32)
    key = jnp.where(bits < 0, bits ^ jnp.int32(0x7FFFFFFF), bits)
    key_ref[...] = jnp.where(causal, key, jnp.int32(INT_MIN))

    def radix(i, thr):
        cand = thr ^ jnp.left_shift(jnp.int32(1), 31 - i)
        cnt = jnp.sum((key_ref[...] >= cand).astype(_F32), axis=1, keepdims=True)
        return jnp.where(cnt >= topk, cand, thr)

    thr = lax.fori_loop(0, 32, radix, jnp.full((tq, 1), INT_MIN, jnp.int32))

    key = key_ref[...]
    above = key > thr
    tie = key == thr
    need = topk - jnp.sum(above.astype(_F32), axis=1, keepdims=True)
    tri = (lax.broadcasted_iota(jnp.int32, (LANES, LANES), 0)
           <= lax.broadcasted_iota(jnp.int32, (LANES, LANES), 1)).astype(_BF16)
    run = jnp.zeros((tq, 1), _F32)
    per = tk // LANES
    for c in range(seq // LANES):
        sl = slice(c * LANES, (c + 1) * LANES)
        tie_c = tie[:, sl]
        prefix = _dot(tie_c.astype(_F32).astype(_BF16), tri) + run
        run = prefix[:, LANES - 1:LANES]
        chosen = (above[:, sl] | (tie_c & (prefix <= need))) & causal[:, sl]
        mask_ref[c // per, :, (c % per) * LANES:(c % per + 1) * LANES] = chosen.astype(_F32)

    for hh in range(DSA_HEADS):
        ql = _dot(qb_ref[0, :, hh * HEAD_DIM:(hh + 1) * HEAD_DIM], wuk_ref[hh]) * ATTN_SCALE
        qlat_ref[hh * tq:(hh + 1) * tq, :] = ql.astype(_BF16)

    row = lax.broadcasted_iota(jnp.int32, (rows, 1), 0)
    t8 = q0 + (row & (tq - 1))
    h8 = row >> (tq.bit_length() - 1)
    slope8 = jnp.zeros((rows, 1), _F32)
    for hh in range(DSA_HEADS):
        slope8 = jnp.where(h8 == hh, slopes_ref[hh], slope8)

    m_ref[...] = jnp.full((rows, 1), NEG_BIG, _F32)
    l_ref[...] = jnp.zeros((rows, 1), _F32)
    acc_ref[...] = jnp.zeros((rows, KV_LATENT), _F32)

    def attend(kc, _):
        start = pl.multiple_of(kc * tk, tk)
        c = cn_ref[0, pl.ds(start, tk), :]
        dist = (t8 - (start + lax.broadcasted_iota(jnp.int32, (1, tk), 1))).astype(_F32)
        mk = mask_ref[kc]
        valid = jnp.concatenate([mk] * DSA_HEADS, axis=0) > 0.5
        s = _dot_nt(qlat_ref[...], c) - slope8 * dist
        s = jnp.where(valid, s, NEG_BIG)
        m_old = m_ref[...]
        m_new = jnp.maximum(m_old, jnp.max(s, axis=1, keepdims=True))
        p = jnp.where(valid, jnp.exp(s - m_new), 0.0)
        alpha = jnp.exp(m_old - m_new)
        l_ref[...] = alpha * l_ref[...] + jnp.sum(p, axis=1, keepdims=True)
        acc_ref[...] = alpha * acc_ref[...] + _dot(p.astype(_BF16), c)
        m_ref[...] = m_new
        return 0

    lax.fori_loop(0, (q0 + tq - 1) // tk + 1, attend, 0)

    l = l_ref[...]
    o_lat = (acc_ref[...] * jnp.where(l > 0, 1.0 / l, 0.0)).astype(_BF16)
    for hh in range(DSA_HEADS):
        o = _dot(o_lat[hh * tq:(hh + 1) * tq], wuv_ref[hh])
        o_ref[0, :, hh * HEAD_DIM:(hh + 1) * HEAD_DIM] = o.astype(o_ref.dtype)


def _dsa(slopes, h1, h2, cn, wuk_t, wuv, seq):
    b = h1.shape[0]
    tq = QUERY_BLOCK
    tk = min(DSA_TK, seq)
    topk = min(DSA_TOPK_MAX, seq // 4)
    width = DSA_HEADS * HEAD_DIM
    rows = DSA_HEADS * tq
    return pl.pallas_call(
        functools.partial(_dsa_kernel, topk=topk),
        grid=(b, seq // tq),
        in_specs=[pl.BlockSpec(memory_space=pltpu.SMEM),
                  pl.BlockSpec((1, tq, width), lambda i, t: (i, t, H1_Q_B // width)),
                  pl.BlockSpec((1, tq, IDX_HEADS * LANES),
                               lambda i, t: (i, t, H1_Q_IDX // (IDX_HEADS * LANES))),
                  pl.BlockSpec((1, seq, LANES), lambda i, t: (i, 0, H1_K_IDX // LANES)),
                  pl.BlockSpec((1, tq, LANES), lambda i, t: (i, t, H2_WIDX // LANES)),
                  pl.BlockSpec((1, seq, KV_LATENT), lambda i, t: (i, 0, 0)),
                  pl.BlockSpec((DSA_HEADS, HEAD_DIM, KV_LATENT), lambda i, t: (0, 0, 0)),
                  pl.BlockSpec((DSA_HEADS, KV_LATENT, HEAD_DIM), lambda i, t: (0, 0, 0))],
        out_specs=pl.BlockSpec((1, tq, width), lambda i, t: (i, t, 0)),
        out_shape=jax.ShapeDtypeStruct((b, seq, width), _BF16),
        scratch_shapes=[pltpu.VMEM((tq, seq), jnp.int32),
                        pltpu.VMEM((seq // tk, tq, tk), _F32),
                        pltpu.VMEM((rows, KV_LATENT), _BF16),
                        pltpu.VMEM((rows, 1), _F32),
                        pltpu.VMEM((rows, 1), _F32),
                        pltpu.VMEM((rows, KV_LATENT), _F32)],
        compiler_params=_cparams(2, 48),
        name="dsa_attention",
    )(slopes, h1, h1, h1, h2, cn, wuk_t, wuv)


def _gmlp_kernel(uv_ref, g_ref, b_ref, ws_ref, bs_ref, o_ref):
    z = jax.nn.gelu(uv_ref[0])
    u = z[:, :GMLP_WIDTH]
    v = z[:, GMLP_WIDTH:]
    mu = jnp.mean(v, axis=-1, keepdims=True)
    var = jnp.mean(jnp.square(v - mu), axis=-1, keepdims=True)
    vn = ((v - mu) * lax.rsqrt(var + LN_EPS) * g_ref[...] + b_ref[...]).astype(_BF16)
    t = ws_ref.shape[1]
    causal = (lax.broadcasted_iota(jnp.int32, (t, t), 0) >= lax.broadcasted_iota(jnp.int32, (t, t), 1))
    for g in range(GMLP_GROUPS):
        sl = slice(g * GMLP_GROUP_DIM, (g + 1) * GMLP_GROUP_DIM)
        w = jnp.where(causal, ws_ref[g], 0.0).astype(_BF16)
        mixed = _dot(w, vn[:, sl]) + bs_ref[:, g:g + 1]
        o_ref[0, :, sl] = (u[:, sl] * mixed).astype(o_ref.dtype)


def _gmlp(h2, ln_g, ln_b, w_s, b_s_t, seq):
    b = h2.shape[0]
    t = GMLP_CHUNK
    return pl.pallas_call(
        _gmlp_kernel,
        grid=(b, seq // t),
        in_specs=[pl.BlockSpec((1, t, 2 * GMLP_WIDTH), lambda i, c: (i, c, H2_UV // (2 * GMLP_WIDTH))),
                  pl.BlockSpec((1, GMLP_WIDTH), lambda i, c: (0, 0)),
                  pl.BlockSpec((1, GMLP_WIDTH), lambda i, c: (0, 0)),
                  pl.BlockSpec((GMLP_GROUPS, t, t), lambda i, c: (0, 0, 0)),
                  pl.BlockSpec((t, GMLP_GROUPS), lambda i, c: (0, 0))],
        out_specs=pl.BlockSpec((1, t, GMLP_WIDTH), lambda i, c: (i, c, 0)),
        out_shape=jax.ShapeDtypeStruct((b, seq, GMLP_WIDTH), _BF16),
        compiler_params=_cparams(2, 32),
        name="gmlp",
    )(h2, ln_g, ln_b, w_s, b_s_t)


def _merge_kernel(oa_ref, ob_ref, oc_ref, ga_ref, gb_ref, gc_ref, wb_ref, o_ref):
    y = ga_ref[...].astype(_F32) * _dot(oa_ref[...], wb_ref[0])
    y = y + gb_ref[...].astype(_F32) * _dot(ob_ref[...], wb_ref[1])
    y = y + gc_ref[...].astype(_F32) * _dot(oc_ref[...], wb_ref[2])
    o_ref[...] = y.astype(o_ref.dtype)


def _merge(o_a, o_b, o_c, gates, w_branch):
    n_tok, width = o_a.shape
    d = w_branch.shape[2]
    tm = min(MERGE_TM, n_tok)
    tn = MERGE_TN
    nj = d // tn

    def gate_spec(n):
        return pl.BlockSpec((tm, tn), lambda i, j: (i, n * nj + j))

    branch = pl.BlockSpec((tm, width), lambda i, j: (i, 0))
    return pl.pallas_call(
        _merge_kernel,
        grid=(n_tok // tm, nj),
        in_specs=[branch, branch, branch, gate_spec(0), gate_spec(1), gate_spec(2),
                  pl.BlockSpec((N_BRANCHES, width, tn), lambda i, j: (0, 0, j))],
        out_specs=pl.BlockSpec((tm, tn), lambda i, j: (i, j)),
        out_shape=jax.ShapeDtypeStruct((n_tok, d), _BF16),
        compiler_params=_cparams(2, 48),
        name="branch_merge",
    )(o_a, o_b, o_c, gates, gates, gates, w_branch)


def _layer_norm(y, g, b):
    mu = jnp.mean(y, axis=-1, keepdims=True)
    var = jnp.mean(jnp.square(y - mu), axis=-1, keepdims=True)
    return (y - mu) * lax.rsqrt(var + LN_EPS) * g + b


def _out_router_kernel(gs_ref, wo_ref, x_ref, g_ref, b_ref, wr_ref, br_ref, x1_ref, rt_ref, *, alpha):
    x1 = _layer_norm(alpha * x_ref[...] + _dot(gs_ref[...], wo_ref[...]), g_ref[...], b_ref[...])
    x1_ref[...] = x1

    logits = jnp.dot(x1, wr_ref[...], preferred_element_type=_F32,
                     precision=lax.Precision.HIGHEST) + br_ref[...]
    gl = [logits[:, j:j + 1] for j in range(N_GROUPS)]
    gmax = functools.reduce(jnp.maximum, gl)
    gi = jnp.full_like(gmax, N_GROUPS - 1)
    for j in reversed(range(N_GROUPS - 1)):
        gi = jnp.where(gl[j] == gmax, float(j), gi)
    gp = 1.0 / functools.reduce(jnp.add, [jnp.exp(v - gmax) for v in gl])
    el = []
    for k in range(EXPERTS_PER_GROUP):
        v = jnp.zeros_like(gmax)
        for j in range(N_GROUPS):
            c = N_GROUPS + j * EXPERTS_PER_GROUP + k
            v = jnp.where(gi == float(j), logits[:, c:c + 1], v)
        el.append(v)
    e1 = functools.reduce(jnp.maximum, el)
    i1 = jnp.full_like(e1, EXPERTS_PER_GROUP - 1)
    for k in reversed(range(EXPERTS_PER_GROUP - 1)):
        i1 = jnp.where(el[k] == e1, float(k), i1)
    rest = [jnp.where(i1 == float(k), -jnp.inf, el[k]) for k in range(EXPERTS_PER_GROUP)]
    e2 = functools.reduce(jnp.maximum, rest)
    i2 = jnp.full_like(e2, EXPERTS_PER_GROUP - 1)
    for k in reversed(range(EXPERTS_PER_GROUP - 1)):
        i2 = jnp.where((rest[k] == e2) & (i1 != float(k)), float(k), i2)
    ex = jnp.exp(e2 - e1)
    w1 = gp / (1.0 + ex)
    w2 = gp * ex / (1.0 + ex)
    lane = lax.broadcasted_iota(jnp.int32, logits.shape, 1)
    rt = jnp.where(lane == 0, gi, 0.0)
    for k in range(EXPERTS_PER_GROUP):
        wk = jnp.where(i1 == float(k), w1, 0.0) + jnp.where(i2 == float(k), w2, 0.0)
        rt = jnp.where(lane == k + 1, wk, rt)
    rt_ref[...] = rt


def _out_router(gs, w_out, x, ln_g, ln_b, w_r, b_r, alpha):
    n_tok, d = x.shape
    tm = min(OUT_TM, n_tok)
    row = pl.BlockSpec((tm, d), lambda i: (i, 0))
    vec = pl.BlockSpec((1, d), lambda i: (0, 0))
    return pl.pallas_call(
        functools.partial(_out_router_kernel, alpha=alpha),
        grid=(n_tok // tm,),
        in_specs=[row, pl.BlockSpec((d, d), lambda i: (0, 0)), row, vec, vec,
                  pl.BlockSpec((d, LANES), lambda i: (0, 0)),
                  pl.BlockSpec((1, LANES), lambda i: (0, 0))],
        out_specs=[row, pl.BlockSpec((tm, LANES), lambda i: (i, 0))],
        out_shape=[jax.ShapeDtypeStruct((n_tok, d), _F32),
                   jax.ShapeDtypeStruct((n_tok, LANES), _F32)],
        compiler_params=_cparams(1, 48),
        name="out_proj_ln_router",
    )(gs, w_out, x, ln_g, ln_b, w_r, b_r)


def _moe_kernel(tile_group_ref, tile_rows_ref, row_token_ref, x_hbm, cw_ref, wg_ref, wu_ref, wd_ref,
                g_ref, b_ref, o_hbm, xg_ref, xb_ref, acc_ref, y_ref, sem, *, alpha):
    i = pl.program_id(0)
    e = pl.program_id(1)
    tm = xg_ref.shape[0]
    n_rows = tile_rows_ref[i]
    base = i * tm

    def row_copy_in(r):
        tok = row_token_ref[base + r]
        return pltpu.make_async_copy(x_hbm.at[pl.ds(tok, 1)], xg_ref.at[pl.ds(r, 1)], sem.at[0])

    def row_copy_out(r):
        tok = row_token_ref[base + r]
        return pltpu.make_async_copy(y_ref.at[pl.ds(r, 1)], o_hbm.at[pl.ds(tok, 1)], sem.at[1])

    @pl.when((e == 0) & (n_rows > 0))
    def _():
        def start(r, _):
            row_copy_in(r).start()
            return 0

        def wait(r, _):
            row_copy_in(r).wait()
            return 0

        lax.fori_loop(0, tm, start, 0)
        lax.fori_loop(0, tm, wait, 0)
        xb_ref[...] = xg_ref[...].astype(_BF16)

    @pl.when(n_rows > 0)
    def _():
        xb = xb_ref[...]
        lane = lax.broadcasted_iota(jnp.int32, cw_ref.shape, 1)
        cw = jnp.sum(jnp.where(lane == e + 1, cw_ref[...], 0.0), axis=1, keepdims=True)
        hid = jax.nn.silu(_dot(xb, wg_ref[0])) * _dot(xb, wu_ref[0]) * cw
        contrib = _dot(hid.astype(_BF16), wd_ref[0])

        @pl.when(e == 0)
        def _():
            acc_ref[...] = contrib

        @pl.when(e > 0)
        def _():
            acc_ref[...] += contrib

    @pl.when((e == EXPERTS_PER_GROUP - 1) & (n_rows > 0))
    def _():
        y_ref[...] = _layer_norm(alpha * xg_ref[...] + acc_ref[...], g_ref[...], b_ref[...])

        def start(r, _):
            row_copy_out(r).start()
            return 0

        def wait(r, _):
            row_copy_out(r).wait()
            return 0

        lax.fori_loop(0, n_rows, start, 0)
        lax.fori_loop(0, n_rows, wait, 0)


def _moe(x1, rt, wg, wu, wd, ln_g, ln_b, alpha):
    n_tok, d = x1.shape
    tm = min(MOE_TM, n_tok)
    n_tiles = n_tok // tm + N_GROUPS
    n_rows = n_tiles * tm

    gi = rt[:, 0].astype(jnp.int32)
    onehot = (gi[:, None] == jnp.arange(N_GROUPS)[None, :]).astype(jnp.int32)
    rank = jnp.sum((jnp.cumsum(onehot, axis=0) - onehot) * onehot, axis=1)
    counts = jnp.sum(onehot, axis=0)
    padded = (counts + tm - 1) // tm * tm
    ends = jnp.cumsum(padded)
    starts = ends - padded
    dest = starts[gi] + rank
    row_token = jnp.zeros((n_rows,), jnp.int32).at[dest].set(jnp.arange(n_tok, dtype=jnp.int32))
    tile_start = jnp.arange(n_tiles, dtype=jnp.int32) * tm
    tile_group = jnp.minimum(jnp.searchsorted(ends, tile_start, side="right"), N_GROUPS - 1).astype(jnp.int32)
    tile_rows = jnp.clip(starts[tile_group] + counts[tile_group] - tile_start, 0, tm).astype(jnp.int32)
    cw_sorted = rt[row_token]

    def expert(i, e, tg, tr, rtok):
        return tg[i] * EXPERTS_PER_GROUP + e

    grid_spec = pltpu.PrefetchScalarGridSpec(
        num_scalar_prefetch=3,
        grid=(n_tiles, EXPERTS_PER_GROUP),
        in_specs=[pl.BlockSpec(memory_space=pl.ANY),
                  pl.BlockSpec((tm, LANES), lambda i, e, tg, tr, rtok: (i, 0)),
                  pl.BlockSpec((1, d, D_EXPERT), lambda i, e, tg, tr, rtok: (expert(i, e, tg, tr, rtok), 0, 0)),
                  pl.BlockSpec((1, d, D_EXPERT), lambda i, e, tg, tr, rtok: (expert(i, e, tg, tr, rtok), 0, 0)),
                  pl.BlockSpec((1, D_EXPERT, d), lambda i, e, tg, tr, rtok: (expert(i, e, tg, tr, rtok), 0, 0)),
                  pl.BlockSpec((1, d), lambda i, e, tg, tr, rtok: (0, 0)),
                  pl.BlockSpec((1, d), lambda i, e, tg, tr, rtok: (0, 0))],
        out_specs=pl.BlockSpec(memory_space=pl.ANY),
        scratch_shapes=[pltpu.VMEM((tm, d), _F32),
                        pltpu.VMEM((tm, d), _BF16),
                        pltpu.VMEM((tm, d), _F32),
                        pltpu.VMEM((tm, d), _F32),
                        pltpu.SemaphoreType.DMA((2,))],
    )
    return pl.pallas_call(
        functools.partial(_moe_kernel, alpha=alpha),
        grid_spec=grid_spec,
        out_shape=jax.ShapeDtypeStruct((n_tok, d), _F32),
        compiler_params=_cparams(2, 48),
        name="moe_ln2",
    )(tile_group, tile_rows, row_token, x1, cw_sorted, wg, wu, wd, ln_g, ln_b)


def _pack_w_in(w):
    d = w.shape[0]
    sizes = (NSA_HEADS * HEAD_DIM,) + (NSA_KV_HEADS * HEAD_DIM,) * 6 + (
        NSA_HEADS * 3, DSA_HEADS * HEAD_DIM, KV_LATENT, IDX_HEADS * IDX_DIM, IDX_DIM, IDX_HEADS,
        2 * GMLP_WIDTH, N_BRANCHES * d)
    parts, start = [], 0
    for n in sizes:
        parts.append(w[:, start:start + n])
        start += n
    (q_a, k_c, v_c, k_s, v_s, k_w, v_w, g_nsa, q_b, c_kv, q_idx, k_idx, w_idx, uv, g_merge) = parts

    def pad_to(a, n):
        return jnp.pad(a, ((0, 0), (0, n - a.shape[1])))

    q_idx_p = jnp.concatenate([pad_to(q_idx[:, h * IDX_DIM:(h + 1) * IDX_DIM], LANES)
                               for h in range(IDX_HEADS)], axis=1)
    w1 = jnp.concatenate([q_a, k_s, v_s, k_w, v_w, q_b, q_idx_p, pad_to(k_idx, LANES)], axis=1)
    w1 = pad_to(w1, H1_COLS).astype(_BF16)
    per_head = NSA_GROUP * 3
    gn = jnp.concatenate([pad_to(g_nsa[:, h * per_head:(h + 1) * per_head], LANES)
                          for h in range(NSA_KV_HEADS)], axis=1)
    w2 = jnp.concatenate([uv, k_c, v_c, c_kv, gn, pad_to(w_idx, LANES)], axis=1).astype(_BF16)
    return w1, w2, g_merge.astype(_BF16)


def kernel(x, w_in, cmp_w1, cmp_w2, cmp_pe, w_uk, w_uv, kv_norm_g, gmlp_ln_g, gmlp_ln_b, gmlp_w_s, gmlp_b_s, w_branch, w_out, ln1_g, ln1_b, router_group_w, router_group_b, router_expert_w, router_expert_b, expert_w_gate, expert_w_up, expert_w_down, ln2_g, ln2_b):
    b, seq, d = x.shape
    depth = w_in.shape[0]
    alpha = float((2 * depth) ** 0.25)
    n_tok = b * seq
    nc = seq // CMP_STRIDE
    slopes = 2.0 ** (-8.0 * jnp.arange(1, NSA_HEADS + 1, dtype=_F32) / NSA_HEADS)

    xt = x.reshape(n_tok, d)
    for l in range(depth):
        w1, w2, w4 = _pack_w_in(w_in[l])
        h1 = _matmul(xt, w1, _BF16, 768).reshape(b, seq, H1_COLS)
        h2 = _matmul(xt, w2, _F32, 640).reshape(b, seq, H2_COLS)
        gates = _matmul(xt, w4, _BF16, 768, act="sigmoid")

        kv_c = h2[:, :, H2_KC:H2_KC + 2 * NSA_KV_HEADS * HEAD_DIM]
        kv_c = kv_c.reshape(b, nc, CMP_STRIDE, 2, NSA_KV_HEADS, HEAD_DIM).transpose(0, 3, 4, 1, 2, 5)
        kv_c = kv_c.reshape(b, 2, NSA_KV_HEADS, nc, CMP_STRIDE * HEAD_DIM)
        pe = cmp_pe[l].reshape(2, 2, CMP_STRIDE * HEAD_DIM)
        kcv = _compress(kv_c, pe, cmp_w1[l].astype(_BF16), cmp_w2[l].astype(_BF16))
        ocg, selmask = _cmp_select(slopes, h1, kcv, h2, seq)
        o_a = _sel_win(slopes, h1, selmask, ocg, h2, seq)

        cn = _rms_norm(h2, kv_norm_g[l].reshape(1, KV_LATENT), seq)
        o_b = _dsa(slopes, h1, h2, cn, w_uk[l].transpose(0, 2, 1).astype(_BF16), w_uv[l].astype(_BF16), seq)

        o_c = _gmlp(h2, gmlp_ln_g[l].reshape(1, GMLP_WIDTH), gmlp_ln_b[l].reshape(1, GMLP_WIDTH),
                    gmlp_w_s[l], gmlp_b_s[l].T, seq)

        gs = _merge(o_a.reshape(n_tok, -1), o_b.reshape(n_tok, -1), o_c.reshape(n_tok, -1), gates,
                    w_branch[l].astype(_BF16))
        w_r = jnp.pad(jnp.concatenate([router_group_w[l], router_expert_w[l]], axis=1),
                      ((0, 0), (0, LANES - N_GROUPS - N_EXPERTS)))
        b_r = jnp.pad(jnp.concatenate([router_group_b[l], router_expert_b[l]]),
                      (0, LANES - N_GROUPS - N_EXPERTS)).reshape(1, LANES)
        x1, rt = _out_router(gs, w_out[l].astype(_BF16), xt, ln1_g[l].reshape(1, d), ln1_b[l].reshape(1, d),
                             w_r, b_r, alpha)
        xt = _moe(x1, rt, expert_w_gate[l].astype(_BF16), expert_w_up[l].astype(_BF16),
                  expert_w_down[l].astype(_BF16), ln2_g[l].reshape(1, d), ln2_b[l].reshape(1, d), alpha)
    return xt.reshape(b, seq, d)
```

```python
import functools

import jax
import jax.numpy as jnp
from jax import lax
from jax.experimental import pallas as pl
from jax.experimental.pallas import tpu as pltpu

HEAD_DIM = 128
NSA_HEADS = 8
NSA_KV_HEADS = 2
NSA_GROUP = NSA_HEADS // NSA_KV_HEADS
CMP_BLOCK = 32
CMP_STRIDE = 16
SEL_BLOCK = 64
N_SEL = 16
WINDOW = 512
FORCE_BONUS = 1e4
DSA_HEADS = 8
KV_LATENT = 256
IDX_HEADS = 4
IDX_DIM = 64
DSA_TOPK_MAX = 256
QUERY_BLOCK = 128
GMLP_GROUPS = 8
GMLP_GROUP_DIM = 128
GMLP_WIDTH = GMLP_GROUPS * GMLP_GROUP_DIM
GMLP_CHUNK = 128
N_BRANCHES = 3
N_GROUPS = 4
EXPERTS_PER_GROUP = 4
N_EXPERTS = N_GROUPS * EXPERTS_PER_GROUP
D_EXPERT = 512
LN_EPS = 1e-5
NEG_BIG = -1e30
ATTN_SCALE = HEAD_DIM ** -0.5

LANES = 128
INT_MIN = -2 ** 31
MIB = 1024 * 1024

H1_Q_A = 0
H1_KS, H1_VS, H1_KW, H1_VW = 1024, 1280, 1536, 1792
H1_Q_B = 2048
H1_Q_IDX = 3072
H1_K_IDX = 3584
H1_COLS = 3840
H2_UV = 0
H2_KC, H2_VC = 2048, 2304
H2_CKV = 2560
H2_GN = 2816
H2_WIDX = 3072
H2_COLS = 3200

MM_TM = 1024
CMP_TQ = 256
SEL_TK = 256
ATT_TQ = 128
WIN_TK = 128
DSA_TK = 512
MERGE_TM = 512
MERGE_TN = 512
OUT_TM = 512
MOE_TM = 512

_F32 = jnp.float32
_BF16 = jnp.bfloat16


def _cparams(n_axes, vmem_mib):
    return pltpu.CompilerParams(dimension_semantics=("arbitrary",) * n_axes,
                                vmem_limit_bytes=vmem_mib * MIB)


def _dot(a, b):
    return jnp.dot(a, b, preferred_element_type=_F32)


def _dot_nt(a, b):
    return lax.dot_general(a, b, (((1,), (1,)), ((), ())), preferred_element_type=_F32)


def _mm_kernel(a_ref, b_ref, o_ref, a_bf_ref, *, act):
    @pl.when(pl.program_id(1) == 0)
    def _():
        a_bf_ref[...] = a_ref[...].astype(_BF16)

    acc = _dot(a_bf_ref[...], b_ref[...])
    if act == "sigmoid":
        acc = jax.nn.sigmoid(acc)
    o_ref[...] = acc.astype(o_ref.dtype)


def _matmul(a, w, layer, out_dtype, tn, act=None):
    m, k = a.shape
    n = w.shape[2]
    tm = min(MM_TM, m)
    return pl.pallas_call(
        functools.partial(_mm_kernel, act=act),
        grid=(m // tm, n // tn),
        in_specs=[pl.BlockSpec((tm, k), lambda i, j: (i, 0)),
                  pl.BlockSpec((None, k, tn), lambda i, j: (layer, 0, j))],
        out_specs=pl.BlockSpec((tm, tn), lambda i, j: (i, j)),
        out_shape=jax.ShapeDtypeStruct((m, n), out_dtype),
        scratch_shapes=[pltpu.VMEM((tm, k), _BF16)],
        compiler_params=_cparams(2, 48),
        name="proj_matmul",
    )(a, w)


def _compress_kernel(x_ref, pe_ref, w1_ref, w2_ref, o_ref):
    nc = o_ref.shape[3]
    top = jnp.zeros((nc, HEAD_DIM), _F32)
    bot = jnp.zeros((nc, HEAD_DIM), _F32)
    for ll in range(CMP_STRIDE):
        y = x_ref[0, pl.ds(ll, nc, stride=CMP_STRIDE), :]
        lo, hi = ll, CMP_STRIDE + ll
        top = top + _dot((y + pe_ref[0, lo:lo + 1, :]).astype(_BF16),
                         w1_ref[0, lo * HEAD_DIM:(lo + 1) * HEAD_DIM, :])
        bot = bot + _dot((y + pe_ref[0, hi:hi + 1, :]).astype(_BF16),
                         w1_ref[0, hi * HEAD_DIM:(hi + 1) * HEAD_DIM, :])
    pre = top + pltpu.roll(bot, nc - 1, 0)
    o_ref[0, 0, 0] = _dot(jax.nn.gelu(pre).astype(_BF16), w2_ref[0]).astype(o_ref.dtype)


def _compress(h2, pe, w1, w2, layer, seq):
    b = h2.shape[0]
    nc = seq // CMP_STRIDE
    col = H2_KC // HEAD_DIM
    return pl.pallas_call(
        _compress_kernel,
        grid=(b, 2, NSA_KV_HEADS),
        in_specs=[pl.BlockSpec((1, seq, HEAD_DIM), lambda i, j, h: (i, 0, col + j * NSA_KV_HEADS + h)),
                  pl.BlockSpec((1, CMP_BLOCK, HEAD_DIM), lambda i, j, h: (2 * layer + j, 0, 0)),
                  pl.BlockSpec((1, CMP_BLOCK * HEAD_DIM, HEAD_DIM), lambda i, j, h: (2 * layer + j, 0, 0)),
                  pl.BlockSpec((1, HEAD_DIM, HEAD_DIM), lambda i, j, h: (2 * layer + j, 0, 0))],
        out_specs=pl.BlockSpec((1, 1, 1, nc, HEAD_DIM), lambda i, j, h: (i, j, h, 0, 0)),
        out_shape=jax.ShapeDtypeStruct((b, 2, NSA_KV_HEADS, nc, HEAD_DIM), _BF16),
        compiler_params=_cparams(3, 32),
        name="nsa_compress",
    )(h2, pe, w1, w2)


def _cmp_select_kernel(slopes_ref, q_ref, kc_ref, vc_ref, gn_ref, ocg_ref, mask_ref, *, seq):
    h = pl.program_id(1)
    qt = pl.program_id(2)
    tq = q_ref.shape[1]
    nc = kc_ref.shape[3]
    n_cmp = (seq - CMP_BLOCK) // CMP_STRIDE + 1
    n_slc = seq // SEL_BLOCK

    t_col = qt * tq + lax.broadcasted_iota(jnp.int32, (tq, 1), 0)
    i_row = lax.broadcasted_iota(jnp.int32, (1, nc), 1)
    dist_i = t_col - (i_row * CMP_STRIDE + (CMP_BLOCK - 1))
    valid = (dist_i >= 0) & (i_row < n_cmp)
    dist = dist_i.astype(_F32)

    kc = kc_ref[0, 0, 0]
    vc = vc_ref[0, 0, 0]
    gn = jax.nn.sigmoid(gn_ref[0])
    psum = jnp.zeros((tq, nc), _F32)
    for g in range(NSA_GROUP):
        qg = q_ref[0, :, g * HEAD_DIM:(g + 1) * HEAD_DIM]
        slope = slopes_ref[h * NSA_GROUP + g]
        s = _dot_nt(qg, kc) * ATTN_SCALE - slope * dist
        s = jnp.where(valid, s, NEG_BIG)
        e = jnp.exp(s - jnp.max(s, axis=1, keepdims=True))
        p = e / jnp.sum(e, axis=1, keepdims=True)
        p = jnp.where(valid, p, 0.0)
        o = _dot(p.astype(_BF16), vc)
        ocg_ref[0, :, g * HEAD_DIM:(g + 1) * HEAD_DIM] = gn[:, 3 * g:3 * g + 1] * o
        psum = psum + p

    ci = lax.broadcasted_iota(jnp.int32, (nc, LANES), 0) * CMP_STRIDE
    sj = lax.broadcasted_iota(jnp.int32, (nc, LANES), 1) * SEL_BLOCK
    overlap = ((ci < sj + SEL_BLOCK) & (ci + CMP_BLOCK > sj)).astype(_F32)
    imp = jnp.dot(psum, overlap, preferred_element_type=_F32, precision=lax.Precision.HIGHEST)

    j_row = lax.broadcasted_iota(jnp.int32, (1, LANES), 1)
    cur = t_col >> (SEL_BLOCK.bit_length() - 1)
    forced = (j_row == 0) | (j_row == cur) | (j_row == cur - 1)
    imp = jnp.where(j_row <= cur, imp + jnp.where(forced, FORCE_BONUS, 0.0), NEG_BIG)
    imp = jnp.where(j_row < n_slc, imp, -3e38)
    rank = jnp.zeros((tq, LANES), _F32)
    for jp in range(n_slc):
        col = imp[:, jp:jp + 1]
        before = (col > imp) | ((col == imp) & (j_row > jp))
        rank = rank + before.astype(_F32)
    sel = (rank < min(N_SEL, n_slc)).astype(_BF16)

    tk = mask_ref.shape[4]
    bj = lax.broadcasted_iota(jnp.int32, (LANES, tk), 0)
    ks = lax.broadcasted_iota(jnp.int32, (LANES, tk), 1)
    for kt in range(mask_ref.shape[2]):
        expand = (bj == ((ks + kt * tk) >> (SEL_BLOCK.bit_length() - 1))).astype(_BF16)
        mask_ref[0, 0, kt] = _dot(sel, expand).astype(mask_ref.dtype)


def _cmp_select(slopes, h1, kcv, h2, seq):
    b = h1.shape[0]
    nc = kcv.shape[3]
    tq = min(CMP_TQ, seq)
    tk = min(SEL_TK, seq)
    nkt = seq // tk
    qw = NSA_GROUP * HEAD_DIM
    return pl.pallas_call(
        functools.partial(_cmp_select_kernel, seq=seq),
        grid=(b, NSA_KV_HEADS, seq // tq),
        in_specs=[pl.BlockSpec(memory_space=pltpu.SMEM),
                  pl.BlockSpec((1, tq, qw), lambda i, h, t: (i, t, h)),
                  pl.BlockSpec((1, 1, 1, nc, HEAD_DIM), lambda i, h, t: (i, 0, h, 0, 0)),
                  pl.BlockSpec((1, 1, 1, nc, HEAD_DIM), lambda i, h, t: (i, 1, h, 0, 0)),
                  pl.BlockSpec((1, tq, LANES), lambda i, h, t: (i, t, H2_GN // LANES + h))],
        out_specs=[pl.BlockSpec((1, tq, qw), lambda i, h, t: (i, t, h)),
                   pl.BlockSpec((1, 1, nkt, tq, tk), lambda i, h, t: (i, h, 0, t, 0))],
        out_shape=[jax.ShapeDtypeStruct((b, seq, NSA_HEADS * HEAD_DIM), _F32),
                   jax.ShapeDtypeStruct((b, NSA_KV_HEADS, nkt, seq, tk), _BF16)],
        compiler_params=_cparams(3, 32),
        name="nsa_cmp_select",
    )(slopes, h1, kcv, kcv, h2)


def _flash(q4, k_ref, v_ref, kt_lo, kt_hi, tk, t4, slope4, mask_fn):
    rows = q4.shape[0]

    def body(kt, carry):
        m, l, acc = carry
        start = pl.multiple_of(kt * tk, tk)
        k = k_ref[0, pl.ds(start, tk), :]
        v = v_ref[0, pl.ds(start, tk), :]
        s_pos = start + lax.broadcasted_iota(jnp.int32, (1, tk), 1)
        dist_i = t4 - s_pos
        valid = mask_fn(kt, dist_i)
        s = _dot_nt(q4, k) * ATTN_SCALE - slope4 * dist_i.astype(_F32)
        s = jnp.where(valid, s, NEG_BIG)
        m_new = jnp.maximum(m, jnp.max(s, axis=1, keepdims=True))
        p = jnp.where(valid, jnp.exp(s - m_new), 0.0)
        alpha = jnp.exp(m - m_new)
        l = alpha * l + jnp.sum(p, axis=1, keepdims=True)
        acc = alpha * acc + _dot(p.astype(_BF16), v)
        return m_new, l, acc

    init = (jnp.full((rows, 1), NEG_BIG, _F32), jnp.zeros((rows, 1), _F32),
            jnp.zeros((rows, HEAD_DIM), _F32))
    _, l, acc = lax.fori_loop(kt_lo, kt_hi + 1, body, init)
    return acc * jnp.where(l > 0, 1.0 / l, 0.0)


def _sel_win_kernel(slopes_ref, q_ref, ks_ref, vs_ref, kw_ref, vw_ref, mask_ref, ocg_ref, gn_ref,
                    o_ref):
    h = pl.program_id(1)
    qt = pl.program_id(2)
    tq = q_ref.shape[1]
    rows = NSA_GROUP * tq
    q0 = qt * tq
    q4 = jnp.concatenate([q_ref[0, :, g * HEAD_DIM:(g + 1) * HEAD_DIM] for g in range(NSA_GROUP)],
                         axis=0)
    row = lax.broadcasted_iota(jnp.int32, (rows, 1), 0)
    t4 = q0 + (row & (tq - 1))
    g4 = row >> (tq.bit_length() - 1)
    slope4 = jnp.zeros((rows, 1), _F32)
    for g in range(NSA_GROUP):
        slope4 = jnp.where(g4 == g, slopes_ref[h * NSA_GROUP + g], slope4)

    sel_tk = mask_ref.shape[4]

    def sel_mask(kt, dist_i):
        mk = mask_ref[0, 0, kt].astype(_F32)
        mk4 = jnp.concatenate([mk] * NSA_GROUP, axis=0)
        return (mk4 > 0.5) & (dist_i >= 0)

    def win_mask(kt, dist_i):
        return (dist_i >= 0) & (dist_i < WINDOW)

    o_slc = _flash(q4, ks_ref, vs_ref, 0, (q0 + tq - 1) // sel_tk, sel_tk, t4, slope4, sel_mask)
    win_tk = min(WIN_TK, ks_ref.shape[1])
    win_lo = jnp.maximum(q0 - (WINDOW - 1), 0) // win_tk
    o_win = _flash(q4, kw_ref, vw_ref, win_lo, (q0 + tq - 1) // win_tk, win_tk, t4, slope4, win_mask)

    gn = jax.nn.sigmoid(gn_ref[0])
    for g in range(NSA_GROUP):
        o = (ocg_ref[0, :, g * HEAD_DIM:(g + 1) * HEAD_DIM]
             + gn[:, 3 * g + 1:3 * g + 2] * o_slc[g * tq:(g + 1) * tq]
             + gn[:, 3 * g + 2:3 * g + 3] * o_win[g * tq:(g + 1) * tq])
        o_ref[0, :, g * HEAD_DIM:(g + 1) * HEAD_DIM] = o.astype(o_ref.dtype)


def _sel_win(slopes, h1, selmask, ocg, h2, seq):
    b = h1.shape[0]
    tq = min(ATT_TQ, seq)
    nkt, tk = selmask.shape[2], selmask.shape[4]
    qw = NSA_GROUP * HEAD_DIM

    def kv_spec(col):
        return pl.BlockSpec((1, seq, HEAD_DIM), lambda i, h, t: (i, 0, col // HEAD_DIM + h))

    return pl.pallas_call(
        _sel_win_kernel,
        grid=(b, NSA_KV_HEADS, seq // tq),
        in_specs=[pl.BlockSpec(memory_space=pltpu.SMEM),
                  pl.BlockSpec((1, tq, qw), lambda i, h, t: (i, t, h)),
                  kv_spec(H1_KS), kv_spec(H1_VS), kv_spec(H1_KW), kv_spec(H1_VW),
                  pl.BlockSpec((1, 1, nkt, tq, tk), lambda i, h, t: (i, h, 0, t, 0)),
                  pl.BlockSpec((1, tq, qw), lambda i, h, t: (i, t, h)),
                  pl.BlockSpec((1, tq, LANES), lambda i, h, t: (i, t, H2_GN // LANES + h))],
        out_specs=pl.BlockSpec((1, tq, qw), lambda i, h, t: (i, t, h)),
        out_shape=jax.ShapeDtypeStruct((b, seq, NSA_HEADS * HEAD_DIM), _BF16),
        compiler_params=_cparams(3, 32),
        name="nsa_sel_win",
    )(slopes, h1, h1, h1, h1, h1, selmask, ocg, h2)


def _rms_kernel(c_ref, g_ref, o_ref):
    c = c_ref[0]
    o_ref[0] = (c * lax.rsqrt(jnp.mean(c * c, axis=-1, keepdims=True) + LN_EPS) * g_ref[...]
                ).astype(o_ref.dtype)


def _rms_norm(h2, g, layer, seq):
    b = h2.shape[0]
    ts = min(512, seq)
    return pl.pallas_call(
        _rms_kernel,
        grid=(b, seq // ts),
        in_specs=[pl.BlockSpec((1, ts, KV_LATENT), lambda i, t: (i, t, H2_CKV // KV_LATENT)),
                  pl.BlockSpec((None, 1, KV_LATENT), lambda i, t: (layer, 0, 0))],
        out_specs=pl.BlockSpec((1, ts, KV_LATENT), lambda i, t: (i, t, 0)),
        out_shape=jax.ShapeDtypeStruct((b, seq, KV_LATENT), _BF16),
        compiler_params=_cparams(2, 32),
        name="dsa_rms_norm",
    )(h2, g)


def _dsa_kernel(slopes_ref, qb_ref, qi_ref, ki_ref, wi_ref, cn_ref, wuk_ref, wuv_ref, o_ref,
                key_ref, mask_ref, qlat_ref, m_ref, l_ref, acc_ref, *, topk):
    qt = pl.program_id(1)
    tq = qb_ref.shape[1]
    seq = ki_ref.shape[1]
    tk = mask_ref.shape[2]
    q0 = qt * tq
    rows = DSA_HEADS * tq

    t_col = q0 + lax.broadcasted_iota(jnp.int32, (tq, 1), 0)
    s_row = lax.broadcasted_iota(jnp.int32, (1, seq), 1)
    causal = s_row <= t_col
    ki = ki_ref[0]
    wi = wi_ref[0]
    score = jnp.zeros((tq, seq), _F32)
    for hh in range(IDX_HEADS):
        rel = jnp.maximum(_dot_nt(qi_ref[0, :, hh * LANES:(hh + 1) * LANES], ki), 0.0)
        score = score + wi[:, hh:hh + 1] * rel
    score = jnp.where(score == 0.0, 0.0, score)
    bits = lax.bitcast_convert_type(score, jnp.int32)
    key = jnp.where(bits < 0, bits ^ jnp.int32(0x7FFFFFFF), bits)
    key_ref[...] = jnp.where(causal, key, jnp.int32(INT_MIN))

    def radix(i, thr):
        cand = thr ^ jnp.left_shift(jnp.int32(1), 31 - i)
        cnt = jnp.sum((key_ref[...] >= cand).astype(_F32), axis=1, keepdims=True)
        return jnp.where(cnt >= topk, cand, thr)

    thr = lax.fori_loop(0, 32, radix, jnp.full((tq, 1), INT_MIN, jnp.int32))

    key = key_ref[...]
    above = key > thr
    tie = key == thr
    need = topk - jnp.sum(above.astype(_F32), axis=1, keepdims=True)
    tri = (lax.broadcasted_iota(jnp.int32, (LANES, LANES), 0)
           <= lax.broadcasted_iota(jnp.int32, (LANES, LANES), 1)).astype(_BF16)
    run = jnp.zeros((tq, 1), _F32)
    per = tk // LANES
    for c in range(seq // LANES):
        sl = slice(c * LANES, (c + 1) * LANES)
        tie_c = tie[:, sl]
        prefix = _dot(tie_c.astype(_F32).astype(_BF16), tri) + run
        run = prefix[:, LANES - 1:LANES]
        chosen = (above[:, sl] | (tie_c & (prefix <= need))) & causal[:, sl]
        mask_ref[c // per, :, (c % per) * LANES:(c % per + 1) * LANES] = chosen.astype(_F32)

    for hh in range(DSA_HEADS):
        ql = _dot(qb_ref[0, :, hh * HEAD_DIM:(hh + 1) * HEAD_DIM], wuk_ref[hh]) * ATTN_SCALE
        qlat_ref[hh * tq:(hh + 1) * tq, :] = ql.astype(_BF16)

    row = lax.broadcasted_iota(jnp.int32, (rows, 1), 0)
    t8 = q0 + (row & (tq - 1))
    h8 = row >> (tq.bit_length() - 1)
    slope8 = jnp.zeros((rows, 1), _F32)
    for hh in range(DSA_HEADS):
        slope8 = jnp.where(h8 == hh, slopes_ref[hh], slope8)

    m_ref[...] = jnp.full((rows, 1), NEG_BIG, _F32)
    l_ref[...] = jnp.zeros((rows, 1), _F32)
    acc_ref[...] = jnp.zeros((rows, KV_LATENT), _F32)

    def attend(kc, _):
        start = pl.multiple_of(kc * tk, tk)
        c = cn_ref[0, pl.ds(start, tk), :]
        dist = (t8 - (start + lax.broadcasted_iota(jnp.int32, (1, tk), 1))).astype(_F32)
        mk = mask_ref[kc]
        valid = jnp.concatenate([mk] * DSA_HEADS, axis=0) > 0.5
        s = _dot_nt(qlat_ref[...], c) - slope8 * dist
        s = jnp.where(valid, s, NEG_BIG)
        m_old = m_ref[...]
        m_new = jnp.maximum(m_old, jnp.max(s, axis=1, keepdims=True))
        p = jnp.where(valid, jnp.exp(s - m_new), 0.0)
        alpha = jnp.exp(m_old - m_new)
        l_ref[...] = alpha * l_ref[...] + jnp.sum(p, axis=1, keepdims=True)
        acc_ref[...] = alpha * acc_ref[...] + _dot(p.astype(_BF16), c)
        m_ref[...] = m_new
        return 0

    lax.fori_loop(0, (q0 + tq - 1) // tk + 1, attend, 0)

    l = l_ref[...]
    o_lat = (acc_ref[...] * jnp.where(l > 0, 1.0 / l, 0.0)).astype(_BF16)
    for hh in range(DSA_HEADS):
        o = _dot(o_lat[hh * tq:(hh + 1) * tq], wuv_ref[hh])
        o_ref[0, :, hh * HEAD_DIM:(hh + 1) * HEAD_DIM] = o.astype(o_ref.dtype)


def _dsa(slopes, h1, h2, cn, wuk_t, wuv, layer, seq):
    b = h1.shape[0]
    tq = QUERY_BLOCK
    tk = min(DSA_TK, seq)
    topk = min(DSA_TOPK_MAX, seq // 4)
    width = DSA_HEADS * HEAD_DIM
    rows = DSA_HEADS * tq
    return pl.pallas_call(
        functools.partial(_dsa_kernel, topk=topk),
        grid=(b, seq // tq),
        in_specs=[pl.BlockSpec(memory_space=pltpu.SMEM),
                  pl.BlockSpec((1, tq, width), lambda i, t: (i, t, H1_Q_B // width)),
                  pl.BlockSpec((1, tq, IDX_HEADS * LANES),
                               lambda i, t: (i, t, H1_Q_IDX // (IDX_HEADS * LANES))),
                  pl.BlockSpec((1, seq, LANES), lambda i, t: (i, 0, H1_K_IDX // LANES)),
                  pl.BlockSpec((1, tq, LANES), lambda i, t: (i, t, H2_WIDX // LANES)),
                  pl.BlockSpec((1, seq, KV_LATENT), lambda i, t: (i, 0, 0)),
                  pl.BlockSpec((DSA_HEADS, HEAD_DIM, KV_LATENT), lambda i, t: (layer, 0, 0)),
                  pl.BlockSpec((DSA_HEADS, KV_LATENT, HEAD_DIM), lambda i, t: (layer, 0, 0))],
        out_specs=pl.BlockSpec((1, tq, width), lambda i, t: (i, t, 0)),
        out_shape=jax.ShapeDtypeStruct((b, seq, width), _BF16),
        scratch_shapes=[pltpu.VMEM((tq, seq), jnp.int32),
                        pltpu.VMEM((seq // tk, tq, tk), _F32),
                        pltpu.VMEM((rows, KV_LATENT), _BF16),
                        pltpu.VMEM((rows, 1), _F32),
                        pltpu.VMEM((rows, 1), _F32),
                        pltpu.VMEM((rows, KV_LATENT), _F32)],
        compiler_params=_cparams(2, 48),
        name="dsa_attention",
    )(slopes, h1, h1, h1, h2, cn, wuk_t, wuv)


def _gmlp_kernel(uv_ref, g_ref, b_ref, ws_ref, bs_ref, o_ref):
    z = jax.nn.gelu(uv_ref[0])
    u = z[:, :GMLP_WIDTH]
    v = z[:, GMLP_WIDTH:]
    mu = jnp.mean(v, axis=-1, keepdims=True)
    var = jnp.mean(jnp.square(v - mu), axis=-1, keepdims=True)
    vn = ((v - mu) * lax.rsqrt(var + LN_EPS) * g_ref[...] + b_ref[...]).astype(_BF16)
    t = ws_ref.shape[1]
    causal = (lax.broadcasted_iota(jnp.int32, (t, t), 0) >= lax.broadcasted_iota(jnp.int32, (t, t), 1))
    for g in range(GMLP_GROUPS):
        sl = slice(g * GMLP_GROUP_DIM, (g + 1) * GMLP_GROUP_DIM)
        w = jnp.where(causal, ws_ref[g], 0.0).astype(_BF16)
        mixed = _dot(w, vn[:, sl]) + bs_ref[:, g:g + 1]
        o_ref[0, :, sl] = (u[:, sl] * mixed).astype(o_ref.dtype)


def _gmlp(h2, ln_g, ln_b, w_s, b_s_t, layer, seq):
    b = h2.shape[0]
    t = GMLP_CHUNK
    return pl.pallas_call(
        _gmlp_kernel,
        grid=(b, seq // t),
        in_specs=[pl.BlockSpec((1, t, 2 * GMLP_WIDTH), lambda i, c: (i, c, H2_UV // (2 * GMLP_WIDTH))),
                  pl.BlockSpec((None, 1, GMLP_WIDTH), lambda i, c: (layer, 0, 0)),
                  pl.BlockSpec((None, 1, GMLP_WIDTH), lambda i, c: (layer, 0, 0)),
                  pl.BlockSpec((GMLP_GROUPS, t, t), lambda i, c: (layer, 0, 0)),
                  pl.BlockSpec((None, t, GMLP_GROUPS), lambda i, c: (layer, 0, 0))],
        out_specs=pl.BlockSpec((1, t, GMLP_WIDTH), lambda i, c: (i, c, 0)),
        out_shape=jax.ShapeDtypeStruct((b, seq, GMLP_WIDTH), _BF16),
        compiler_params=_cparams(2, 32),
        name="gmlp",
    )(h2, ln_g, ln_b, w_s, b_s_t)


def _merge_kernel(oa_ref, ob_ref, oc_ref, ga_ref, gb_ref, gc_ref, wb_ref, o_ref):
    y = ga_ref[...].astype(_F32) * _dot(oa_ref[...], wb_ref[0])
    y = y + gb_ref[...].astype(_F32) * _dot(ob_ref[...], wb_ref[1])
    y = y + gc_ref[...].astype(_F32) * _dot(oc_ref[...], wb_ref[2])
    o_ref[...] = y.astype(o_ref.dtype)


def _merge(o_a, o_b, o_c, gates, w_branch, layer):
    n_tok, width = o_a.shape
    d = w_branch.shape[2]
    tm = min(MERGE_TM, n_tok)
    tn = MERGE_TN
    nj = d // tn

    def gate_spec(n):
        return pl.BlockSpec((tm, tn), lambda i, j: (i, n * nj + j))

    branch = pl.BlockSpec((tm, width), lambda i, j: (i, 0))
    return pl.pallas_call(
        _merge_kernel,
        grid=(n_tok // tm, nj),
        in_specs=[branch, branch, branch, gate_spec(0), gate_spec(1), gate_spec(2),
                  pl.BlockSpec((N_BRANCHES, width, tn), lambda i, j: (layer, 0, j))],
        out_specs=pl.BlockSpec((tm, tn), lambda i, j: (i, j)),
        out_shape=jax.ShapeDtypeStruct((n_tok, d), _BF16),
        compiler_params=_cparams(2, 48),
        name="branch_merge",
    )(o_a, o_b, o_c, gates, gates, gates, w_branch)


def _layer_norm(y, g, b):
    mu = jnp.mean(y, axis=-1, keepdims=True)
    var = jnp.mean(jnp.square(y - mu), axis=-1, keepdims=True)
    return (y - mu) * lax.rsqrt(var + LN_EPS) * g + b


def _out_router_kernel(gs_ref, wo_ref, x_ref, g_ref, b_ref, wr_ref, br_ref, x1_ref, *, alpha):
    d = x_ref.shape[1]
    x1 = _layer_norm(alpha * x_ref[...] + _dot(gs_ref[...], wo_ref[...]), g_ref[...], b_ref[...])
    x1_ref[:, :d] = x1

    logits = jnp.dot(x1, wr_ref[...], preferred_element_type=_F32,
                     precision=lax.Precision.HIGHEST) + br_ref[...]
    gl = [logits[:, j:j + 1] for j in range(N_GROUPS)]
    gmax = functools.reduce(jnp.maximum, gl)
    gi = jnp.full_like(gmax, N_GROUPS - 1)
    for j in reversed(range(N_GROUPS - 1)):
        gi = jnp.where(gl[j] == gmax, float(j), gi)
    gp = 1.0 / functools.reduce(jnp.add, [jnp.exp(v - gmax) for v in gl])
    el = []
    for k in range(EXPERTS_PER_GROUP):
        v = jnp.zeros_like(gmax)
        for j in range(N_GROUPS):
            c = N_GROUPS + j * EXPERTS_PER_GROUP + k
            v = jnp.where(gi == float(j), logits[:, c:c + 1], v)
        el.append(v)
    e1 = functools.reduce(jnp.maximum, el)
    i1 = jnp.full_like(e1, EXPERTS_PER_GROUP - 1)
    for k in reversed(range(EXPERTS_PER_GROUP - 1)):
        i1 = jnp.where(el[k] == e1, float(k), i1)
    rest = [jnp.where(i1 == float(k), -jnp.inf, el[k]) for k in range(EXPERTS_PER_GROUP)]
    e2 = functools.reduce(jnp.maximum, rest)
    i2 = jnp.full_like(e2, EXPERTS_PER_GROUP - 1)
    for k in reversed(range(EXPERTS_PER_GROUP - 1)):
        i2 = jnp.where((rest[k] == e2) & (i1 != float(k)), float(k), i2)
    ex = jnp.exp(e2 - e1)
    w1 = gp / (1.0 + ex)
    w2 = gp * ex / (1.0 + ex)
    lane = lax.broadcasted_iota(jnp.int32, logits.shape, 1)
    rt = jnp.where(lane == 0, gi, 0.0)
    for k in range(EXPERTS_PER_GROUP):
        wk = jnp.where(i1 == float(k), w1, 0.0) + jnp.where(i2 == float(k), w2, 0.0)
        rt = jnp.where(lane == k + 1, wk, rt)
    x1_ref[:, d:] = rt


def _out_router(gs, w_out, x, ln_g, ln_b, w_r, b_r, layer, alpha):
    n_tok, d = x.shape
    tm = min(OUT_TM, n_tok)
    row = pl.BlockSpec((tm, d), lambda i: (i, 0))
    vec = pl.BlockSpec((None, 1, d), lambda i: (layer, 0, 0))
    return pl.pallas_call(
        functools.partial(_out_router_kernel, alpha=alpha),
        grid=(n_tok // tm,),
        in_specs=[row, pl.BlockSpec((None, d, d), lambda i: (layer, 0, 0)), row, vec, vec,
                  pl.BlockSpec((None, d, LANES), lambda i: (layer, 0, 0)),
                  pl.BlockSpec((None, 1, LANES), lambda i: (layer, 0, 0))],
        out_specs=pl.BlockSpec((tm, d + LANES), lambda i: (i, 0)),
        out_shape=jax.ShapeDtypeStruct((n_tok, d + LANES), _F32),
        compiler_params=_cparams(1, 48),
        name="out_proj_ln_router",
    )(gs, w_out, x, ln_g, ln_b, w_r, b_r)


def _moe_kernel(tile_group_ref, tile_rows_ref, row_token_ref, x_hbm, wg_ref, wu_ref, wd_ref,
                g_ref, b_ref, o_hbm, xg_ref, xb_ref, acc_ref, y_ref, sem, *, alpha):
    i = pl.program_id(0)
    e = pl.program_id(1)
    tm = xg_ref.shape[0]
    d = y_ref.shape[1]
    n_rows = tile_rows_ref[i]
    base = i * tm

    def row_copy_in(r):
        tok = row_token_ref[base + r]
        return pltpu.make_async_copy(x_hbm.at[pl.ds(tok, 1)], xg_ref.at[pl.ds(r, 1)], sem.at[0])

    def row_copy_out(r):
        tok = row_token_ref[base + r]
        return pltpu.make_async_copy(y_ref.at[pl.ds(r, 1)], o_hbm.at[pl.ds(tok, 1)], sem.at[1])

    @pl.when((e == 0) & (n_rows > 0))
    def _():
        def start(r, _):
            row_copy_in(r).start()
            return 0

        def wait(r, _):
            row_copy_in(r).wait()
            return 0

        lax.fori_loop(0, tm, start, 0)
        lax.fori_loop(0, tm, wait, 0)
        xb_ref[...] = xg_ref[:, :d].astype(_BF16)

    @pl.when(n_rows > 0)
    def _():
        xb = xb_ref[...]
        route = xg_ref[:, d:]
        lane = lax.broadcasted_iota(jnp.int32, route.shape, 1)
        cw = jnp.sum(jnp.where(lane == e + 1, route, 0.0), axis=1, keepdims=True)
        hid = jax.nn.silu(_dot(xb, wg_ref[0])) * _dot(xb, wu_ref[0]) * cw
        contrib = _dot(hid.astype(_BF16), wd_ref[0])

        @pl.when(e == 0)
        def _():
            acc_ref[...] = contrib

        @pl.when(e > 0)
        def _():
            acc_ref[...] += contrib

    @pl.when((e == EXPERTS_PER_GROUP - 1) & (n_rows > 0))
    def _():
        y_ref[...] = _layer_norm(alpha * xg_ref[:, :d] + acc_ref[...], g_ref[...], b_ref[...])

        def start(r, _):
            row_copy_out(r).start()
            return 0

        def wait(r, _):
            row_copy_out(r).wait()
            return 0

        lax.fori_loop(0, n_rows, start, 0)
        lax.fori_loop(0, n_rows, wait, 0)


def _moe(x1e, wg, wu, wd, ln_g, ln_b, layer, alpha):
    n_tok = x1e.shape[0]
    d = x1e.shape[1] - LANES
    tm = min(MOE_TM, n_tok)
    n_tiles = n_tok // tm + N_GROUPS
    n_rows = n_tiles * tm

    gi = x1e[:, d].astype(jnp.int32)
    onehot = (gi[:, None] == jnp.arange(N_GROUPS)[None, :]).astype(jnp.int32)
    rank = jnp.sum((jnp.cumsum(onehot, axis=0) - onehot) * onehot, axis=1)
    counts = jnp.sum(onehot, axis=0)
    padded = (counts + tm - 1) // tm * tm
    ends = jnp.cumsum(padded)
    starts = ends - padded
    dest = jnp.sum(onehot * starts[None, :], axis=1) + rank
    row_token = jnp.zeros((n_rows,), jnp.int32).at[dest].set(jnp.arange(n_tok, dtype=jnp.int32))
    tile_start = jnp.arange(n_tiles, dtype=jnp.int32) * tm
    in_group = (tile_start[:, None] >= starts[None, :]) & (tile_start[:, None] < ends[None, :])
    tile_group = jnp.sum(in_group * jnp.arange(N_GROUPS)[None, :], axis=1).astype(jnp.int32)
    tile_rows = jnp.sum(in_group * jnp.clip(starts + counts - tile_start[:, None], 0, tm), axis=1)
    tile_rows = tile_rows.astype(jnp.int32)

    def expert(i, e, tg, tr, rtok):
        return layer * N_EXPERTS + tg[i] * EXPERTS_PER_GROUP + e

    def vec(i, e, tg, tr, rtok):
        return (layer, 0, 0)

    grid_spec = pltpu.PrefetchScalarGridSpec(
        num_scalar_prefetch=3,
        grid=(n_tiles, EXPERTS_PER_GROUP),
        in_specs=[pl.BlockSpec(memory_space=pl.ANY),
                  pl.BlockSpec((1, d, D_EXPERT), lambda i, e, tg, tr, rtok: (expert(i, e, tg, tr, rtok), 0, 0)),
                  pl.BlockSpec((1, d, D_EXPERT), lambda i, e, tg, tr, rtok: (expert(i, e, tg, tr, rtok), 0, 0)),
                  pl.BlockSpec((1, D_EXPERT, d), lambda i, e, tg, tr, rtok: (expert(i, e, tg, tr, rtok), 0, 0)),
                  pl.BlockSpec((None, 1, d), vec),
                  pl.BlockSpec((None, 1, d), vec)],
        out_specs=pl.BlockSpec(memory_space=pl.ANY),
        scratch_shapes=[pltpu.VMEM((tm, d + LANES), _F32),
                        pltpu.VMEM((tm, d), _BF16),
                        pltpu.VMEM((tm, d), _F32),
                        pltpu.VMEM((tm, d), _F32),
                        pltpu.SemaphoreType.DMA((2,))],
    )
    return pl.pallas_call(
        functools.partial(_moe_kernel, alpha=alpha),
        grid_spec=grid_spec,
        out_shape=jax.ShapeDtypeStruct((n_tok, d), _F32),
        compiler_params=_cparams(2, 48),
        name="moe_ln2",
    )(tile_group, tile_rows, row_token, x1e, wg, wu, wd, ln_g, ln_b)


def _in_offsets(d):
    names = ("q_a", "k_c", "v_c", "k_s", "v_s", "k_w", "v_w", "g_nsa", "q_b", "c_kv", "q_idx", "k_idx",
             "w_idx", "uv", "g_merge")
    kv = NSA_KV_HEADS * HEAD_DIM
    sizes = (NSA_HEADS * HEAD_DIM, kv, kv, kv, kv, kv, kv, NSA_HEADS * 3, DSA_HEADS * HEAD_DIM, KV_LATENT,
             IDX_HEADS * IDX_DIM, IDX_DIM, IDX_HEADS, 2 * GMLP_WIDTH, N_BRANCHES * d)
    offs, start = {}, 0
    for name, n in zip(names, sizes):
        offs[name] = start
        start += n
    return offs, start


def _pack_kernel(w_ref, o1_ref, o2_ref, o4_ref, *, offs):
    rows = w_ref.shape[1]

    def put(dst, dst_off, src_off, n, slot=None):
        slot = n if slot is None else slot
        piece = w_ref[0, :, src_off:src_off + n].astype(_BF16)
        if slot > n:
            piece = jnp.concatenate([piece, jnp.zeros((rows, slot - n), _BF16)], axis=1)
        dst[0, :, dst_off:dst_off + slot] = piece

    kv = NSA_KV_HEADS * HEAD_DIM
    put(o1_ref, H1_Q_A, offs["q_a"], NSA_HEADS * HEAD_DIM)
    put(o1_ref, H1_KS, offs["k_s"], 4 * kv)
    put(o1_ref, H1_Q_B, offs["q_b"], DSA_HEADS * HEAD_DIM)
    for h in range(IDX_HEADS):
        put(o1_ref, H1_Q_IDX + h * LANES, offs["q_idx"] + h * IDX_DIM, IDX_DIM, LANES)
    put(o1_ref, H1_K_IDX, offs["k_idx"], IDX_DIM, H1_COLS - H1_K_IDX)

    put(o2_ref, H2_UV, offs["uv"], 2 * GMLP_WIDTH)
    put(o2_ref, H2_KC, offs["k_c"], 2 * kv)
    put(o2_ref, H2_CKV, offs["c_kv"], KV_LATENT)
    per_head = NSA_GROUP * 3
    for h in range(NSA_KV_HEADS):
        put(o2_ref, H2_GN + h * LANES, offs["g_nsa"] + h * per_head, per_head, LANES)
    put(o2_ref, H2_WIDX, offs["w_idx"], IDX_HEADS, LANES)

    put(o4_ref, 0, offs["g_merge"], o4_ref.shape[2])


def _pack_w_in(w_in):
    layers, d, cols = w_in.shape
    offs, total = _in_offsets(d)
    assert total == cols
    tr = 128
    return pl.pallas_call(
        functools.partial(_pack_kernel, offs=offs),
        grid=(layers, d // tr),
        in_specs=[pl.BlockSpec((1, tr, cols), lambda l, r: (l, r, 0))],
        out_specs=[pl.BlockSpec((1, tr, H1_COLS), lambda l, r: (l, r, 0)),
                   pl.BlockSpec((1, tr, H2_COLS), lambda l, r: (l, r, 0)),
                   pl.BlockSpec((1, tr, N_BRANCHES * d), lambda l, r: (l, r, 0))],
        out_shape=[jax.ShapeDtypeStruct((layers, d, H1_COLS), _BF16),
                   jax.ShapeDtypeStruct((layers, d, H2_COLS), _BF16),
                   jax.ShapeDtypeStruct((layers, d, N_BRANCHES * d), _BF16)],
        compiler_params=_cparams(2, 48),
        name="pack_w_in",
    )(w_in)


def kernel(x, w_in, cmp_w1, cmp_w2, cmp_pe, w_uk, w_uv, kv_norm_g, gmlp_ln_g, gmlp_ln_b, gmlp_w_s, gmlp_b_s, w_branch, w_out, ln1_g, ln1_b, router_group_w, router_group_b, router_expert_w, router_expert_b, expert_w_gate, expert_w_up, expert_w_down, ln2_g, ln2_b):
    b, seq, d = x.shape
    depth = w_in.shape[0]
    alpha = float((2 * depth) ** 0.25)
    n_tok = b * seq
    slopes = 2.0 ** (-8.0 * jnp.arange(1, NSA_HEADS + 1, dtype=_F32) / NSA_HEADS)

    w1, w2, w4 = _pack_w_in(w_in)
    pe = cmp_pe.reshape(depth * 2, CMP_BLOCK, HEAD_DIM)
    cw1 = cmp_w1.astype(_BF16).reshape(depth * 2, CMP_BLOCK * HEAD_DIM, HEAD_DIM)
    cw2 = cmp_w2.astype(_BF16).reshape(depth * 2, HEAD_DIM, HEAD_DIM)
    wuk_t = w_uk.transpose(0, 1, 3, 2).astype(_BF16).reshape(depth * DSA_HEADS, HEAD_DIM, KV_LATENT)
    wuv = w_uv.astype(_BF16).reshape(depth * DSA_HEADS, KV_LATENT, HEAD_DIM)
    kvg = kv_norm_g.reshape(depth, 1, KV_LATENT)
    g_ln_g = gmlp_ln_g.reshape(depth, 1, GMLP_WIDTH)
    g_ln_b = gmlp_ln_b.reshape(depth, 1, GMLP_WIDTH)
    g_ws = gmlp_w_s.reshape(depth * GMLP_GROUPS, GMLP_CHUNK, GMLP_CHUNK)
    g_bs_t = gmlp_b_s.transpose(0, 2, 1)
    wb = w_branch.astype(_BF16).reshape(depth * N_BRANCHES, w_branch.shape[2], d)
    wo = w_out.astype(_BF16)
    n_route = N_GROUPS + N_EXPERTS
    w_r = jnp.pad(jnp.concatenate([router_group_w, router_expert_w], axis=2),
                  ((0, 0), (0, 0), (0, LANES - n_route)))
    b_r = jnp.pad(jnp.concatenate([router_group_b, router_expert_b], axis=1),
                  ((0, 0), (0, LANES - n_route))).reshape(depth, 1, LANES)
    wg = expert_w_gate.astype(_BF16).reshape(depth * N_EXPERTS, d, D_EXPERT)
    wu = expert_w_up.astype(_BF16).reshape(depth * N_EXPERTS, d, D_EXPERT)
    wd = expert_w_down.astype(_BF16).reshape(depth * N_EXPERTS, D_EXPERT, d)
    l1g, l1b = ln1_g.reshape(depth, 1, d), ln1_b.reshape(depth, 1, d)
    l2g, l2b = ln2_g.reshape(depth, 1, d), ln2_b.reshape(depth, 1, d)

    xt = x.reshape(n_tok, d)
    for l in range(depth):
        h1 = _matmul(xt, w1, l, _BF16, 768).reshape(b, seq, H1_COLS)
        h2 = _matmul(xt, w2, l, _F32, 640).reshape(b, seq, H2_COLS)
        gates = _matmul(xt, w4, l, _BF16, 768, act="sigmoid")

        kcv = _compress(h2, pe, cw1, cw2, l, seq)
        ocg, selmask = _cmp_select(slopes, h1, kcv, h2, seq)
        o_a = _sel_win(slopes, h1, selmask, ocg, h2, seq)

        cn = _rms_norm(h2, kvg, l, seq)
        o_b = _dsa(slopes, h1, h2, cn, wuk_t, wuv, l, seq)

        o_c = _gmlp(h2, g_ln_g, g_ln_b, g_ws, g_bs_t, l, seq)

        gs = _merge(o_a.reshape(n_tok, -1), o_b.reshape(n_tok, -1), o_c.reshape(n_tok, -1), gates, wb, l)
        x1e = _out_router(gs, wo, xt, l1g, l1b, w_r, b_r, l, alpha)
        xt = _moe(x1e, wg, wu, wd, l2g, l2b, l, alpha)
    return xt.reshape(b, seq, d)
```

```python
import functools

import jax
import jax.numpy as jnp
from jax import lax
from jax.experimental import pallas as pl
from jax.experimental.pallas import tpu as pltpu

HEAD_DIM = 128
NSA_HEADS = 8
NSA_KV_HEADS = 2
NSA_GROUP = NSA_HEADS // NSA_KV_HEADS
CMP_BLOCK = 32
CMP_STRIDE = 16
SEL_BLOCK = 64
N_SEL = 16
WINDOW = 512
FORCE_BONUS = 1e4
DSA_HEADS = 8
KV_LATENT = 256
IDX_HEADS = 4
IDX_DIM = 64
DSA_TOPK_MAX = 256
QUERY_BLOCK = 128
GMLP_GROUPS = 8
GMLP_GROUP_DIM = 128
GMLP_WIDTH = GMLP_GROUPS * GMLP_GROUP_DIM
GMLP_CHUNK = 128
N_BRANCHES = 3
N_GROUPS = 4
EXPERTS_PER_GROUP = 4
N_EXPERTS = N_GROUPS * EXPERTS_PER_GROUP
D_EXPERT = 512
LN_EPS = 1e-5
NEG_BIG = -1e30
ATTN_SCALE = HEAD_DIM ** -0.5

LANES = 128
INT_MIN = -2 ** 31
MIB = 1024 * 1024

H1_Q_A = 0
H1_KS, H1_VS, H1_KW, H1_VW = 1024, 1280, 1536, 1792
H1_Q_B = 2048
H1_Q_IDX = 3072
H1_K_IDX = 3584
H1_COLS = 3840
H2_UV = 0
H2_KC, H2_VC = 2048, 2304
H2_CKV = 2560
H2_GN = 2816
H2_WIDX = 3072
H2_COLS = 3200

MM_TM = 1024
CMP_TQ = 256
SEL_TK = 256
ATT_TQ = 128
WIN_TK = 128
DSA_TK = 512
MERGE_TM = 512
MERGE_TN = 512
OUT_TM = 512
MOE_TM = 512

_F32 = jnp.float32
_BF16 = jnp.bfloat16


def _cparams(n_axes, vmem_mib):
    return pltpu.CompilerParams(dimension_semantics=("arbitrary",) * n_axes,
                                vmem_limit_bytes=vmem_mib * MIB)


def _dot(a, b):
    return jnp.dot(a, b, preferred_element_type=_F32)


def _dot_nt(a, b):
    return lax.dot_general(a, b, (((1,), (1,)), ((), ())), preferred_element_type=_F32)


def _mm_kernel(a_ref, b_ref, o_ref, a_bf_ref, *, act):
    @pl.when(pl.program_id(1) == 0)
    def _():
        a_bf_ref[...] = a_ref[...].astype(_BF16)

    acc = _dot(a_bf_ref[...], b_ref[...])
    if act == "sigmoid":
        acc = jax.nn.sigmoid(acc)
    o_ref[...] = acc.astype(o_ref.dtype)


def _matmul(a, w, layer, out_dtype, tn, act=None):
    m, k = a.shape
    n = w.shape[2]
    tm = min(MM_TM, m)
    return pl.pallas_call(
        functools.partial(_mm_kernel, act=act),
        grid=(m // tm, n // tn),
        in_specs=[pl.BlockSpec((tm, k), lambda i, j: (i, 0)),
                  pl.BlockSpec((None, k, tn), lambda i, j: (layer, 0, j))],
        out_specs=pl.BlockSpec((tm, tn), lambda i, j: (i, j)),
        out_shape=jax.ShapeDtypeStruct((m, n), out_dtype),
        scratch_shapes=[pltpu.VMEM((tm, k), _BF16)],
        compiler_params=_cparams(2, 48),
        name="proj_matmul",
    )(a, w)


def _compress_kernel(x_ref, pe_ref, w1_ref, w2_ref, o_ref):
    nc = o_ref.shape[3]
    top = jnp.zeros((nc, HEAD_DIM), _F32)
    bot = jnp.zeros((nc, HEAD_DIM), _F32)
    for ll in range(CMP_STRIDE):
        y = x_ref[0, pl.ds(ll, nc, stride=CMP_STRIDE), :]
        lo, hi = ll, CMP_STRIDE + ll
        top = top + _dot((y + pe_ref[0, lo:lo + 1, :]).astype(_BF16),
                         w1_ref[0, lo * HEAD_DIM:(lo + 1) * HEAD_DIM, :])
        bot = bot + _dot((y + pe_ref[0, hi:hi + 1, :]).astype(_BF16),
                         w1_ref[0, hi * HEAD_DIM:(hi + 1) * HEAD_DIM, :])
    pre = top + pltpu.roll(bot, nc - 1, 0)
    o_ref[0, 0, 0] = _dot(jax.nn.gelu(pre).astype(_BF16), w2_ref[0]).astype(o_ref.dtype)


def _compress(h2, pe, w1, w2, layer, seq):
    b = h2.shape[0]
    nc = seq // CMP_STRIDE
    col = H2_KC // HEAD_DIM
    return pl.pallas_call(
        _compress_kernel,
        grid=(b, 2, NSA_KV_HEADS),
        in_specs=[pl.BlockSpec((1, seq, HEAD_DIM), lambda i, j, h: (i, 0, col + j * NSA_KV_HEADS + h)),
                  pl.BlockSpec((1, CMP_BLOCK, HEAD_DIM), lambda i, j, h: (2 * layer + j, 0, 0)),
                  pl.BlockSpec((1, CMP_BLOCK * HEAD_DIM, HEAD_DIM), lambda i, j, h: (2 * layer + j, 0, 0)),
                  pl.BlockSpec((1, HEAD_DIM, HEAD_DIM), lambda i, j, h: (2 * layer + j, 0, 0))],
        out_specs=pl.BlockSpec((1, 1, 1, nc, HEAD_DIM), lambda i, j, h: (i, j, h, 0, 0)),
        out_shape=jax.ShapeDtypeStruct((b, 2, NSA_KV_HEADS, nc, HEAD_DIM), _BF16),
        compiler_params=_cparams(3, 32),
        name="nsa_compress",
    )(h2, pe, w1, w2)


def _cmp_select_kernel(slopes_ref, q_ref, kc_ref, vc_ref, gn_ref, ocg_ref, mask_ref, *, seq):
    h = pl.program_id(1)
    qt = pl.program_id(2)
    tq = q_ref.shape[1]
    nc = kc_ref.shape[3]
    n_cmp = (seq - CMP_BLOCK) // CMP_STRIDE + 1
    n_slc = seq // SEL_BLOCK

    t_col = qt * tq + lax.broadcasted_iota(jnp.int32, (tq, 1), 0)
    i_row = lax.broadcasted_iota(jnp.int32, (1, nc), 1)
    dist_i = t_col - (i_row * CMP_STRIDE + (CMP_BLOCK - 1))
    valid = (dist_i >= 0) & (i_row < n_cmp)
    dist = dist_i.astype(_F32)

    kc = kc_ref[0, 0, 0]
    vc = vc_ref[0, 0, 0]
    gn = jax.nn.sigmoid(gn_ref[0])
    psum = jnp.zeros((tq, nc), _F32)
    for g in range(NSA_GROUP):
        qg = q_ref[0, :, g * HEAD_DIM:(g + 1) * HEAD_DIM]
        slope = slopes_ref[h * NSA_GROUP + g]
        s = _dot_nt(qg, kc) * ATTN_SCALE - slope * dist
        s = jnp.where(valid, s, NEG_BIG)
        e = jnp.exp(s - jnp.max(s, axis=1, keepdims=True))
        p = e / jnp.sum(e, axis=1, keepdims=True)
        p = jnp.where(valid, p, 0.0)
        o = _dot(p.astype(_BF16), vc)
        ocg_ref[0, :, g * HEAD_DIM:(g + 1) * HEAD_DIM] = gn[:, 3 * g:3 * g + 1] * o
        psum = psum + p

    ci = lax.broadcasted_iota(jnp.int32, (nc, LANES), 0) * CMP_STRIDE
    sj = lax.broadcasted_iota(jnp.int32, (nc, LANES), 1) * SEL_BLOCK
    overlap = ((ci < sj + SEL_BLOCK) & (ci + CMP_BLOCK > sj)).astype(_F32)
    imp = jnp.dot(psum, overlap, preferred_element_type=_F32, precision=lax.Precision.HIGHEST)

    j_row = lax.broadcasted_iota(jnp.int32, (1, LANES), 1)
    cur = t_col >> (SEL_BLOCK.bit_length() - 1)
    forced = (j_row == 0) | (j_row == cur) | (j_row == cur - 1)
    imp = jnp.where(j_row <= cur, imp + jnp.where(forced, FORCE_BONUS, 0.0), NEG_BIG)
    imp = jnp.where(j_row < n_slc, imp, -3e38)
    rank = jnp.zeros((tq, LANES), _F32)
    for jp in range(n_slc):
        col = imp[:, jp:jp + 1]
        before = (col > imp) | ((col == imp) & (j_row > jp))
        rank = rank + before.astype(_F32)
    sel = (rank < min(N_SEL, n_slc)).astype(_BF16)

    tk = mask_ref.shape[4]
    bj = lax.broadcasted_iota(jnp.int32, (LANES, tk), 0)
    ks = lax.broadcasted_iota(jnp.int32, (LANES, tk), 1)
    for kt in range(mask_ref.shape[2]):
        expand = (bj == ((ks + kt * tk) >> (SEL_BLOCK.bit_length() - 1))).astype(_BF16)
        mask_ref[0, 0, kt] = _dot(sel, expand).astype(mask_ref.dtype)


def _cmp_select(slopes, h1, kcv, h2, seq):
    b = h1.shape[0]
    nc = kcv.shape[3]
    tq = min(CMP_TQ, seq)
    tk = min(SEL_TK, seq)
    nkt = seq // tk
    qw = NSA_GROUP * HEAD_DIM
    return pl.pallas_call(
        functools.partial(_cmp_select_kernel, seq=seq),
        grid=(b, NSA_KV_HEADS, seq // tq),
        in_specs=[pl.BlockSpec(memory_space=pltpu.SMEM),
                  pl.BlockSpec((1, tq, qw), lambda i, h, t: (i, t, h)),
                  pl.BlockSpec((1, 1, 1, nc, HEAD_DIM), lambda i, h, t: (i, 0, h, 0, 0)),
                  pl.BlockSpec((1, 1, 1, nc, HEAD_DIM), lambda i, h, t: (i, 1, h, 0, 0)),
                  pl.BlockSpec((1, tq, LANES), lambda i, h, t: (i, t, H2_GN // LANES + h))],
        out_specs=[pl.BlockSpec((1, tq, qw), lambda i, h, t: (i, t, h)),
                   pl.BlockSpec((1, 1, nkt, tq, tk), lambda i, h, t: (i, h, 0, t, 0))],
        out_shape=[jax.ShapeDtypeStruct((b, seq, NSA_HEADS * HEAD_DIM), _F32),
                   jax.ShapeDtypeStruct((b, NSA_KV_HEADS, nkt, seq, tk), _BF16)],
        compiler_params=_cparams(3, 32),
        name="nsa_cmp_select",
    )(slopes, h1, kcv, kcv, h2)


def _flash(q4, k_ref, v_ref, kt_lo, kt_hi, tk, t4, slope4, mask_fn):
    rows = q4.shape[0]

    def body(kt, carry):
        m, l, acc = carry
        start = pl.multiple_of(kt * tk, tk)
        k = k_ref[0, pl.ds(start, tk), :]
        v = v_ref[0, pl.ds(start, tk), :]
        s_pos = start + lax.broadcasted_iota(jnp.int32, (1, tk), 1)
        dist_i = t4 - s_pos
        valid = mask_fn(kt, dist_i)
        s = _dot_nt(q4, k) * ATTN_SCALE - slope4 * dist_i.astype(_F32)
        s = jnp.where(valid, s, NEG_BIG)
        m_new = jnp.maximum(m, jnp.max(s, axis=1, keepdims=True))
        p = jnp.where(valid, jnp.exp(s - m_new), 0.0)
        alpha = jnp.exp(m - m_new)
        l = alpha * l + jnp.sum(p, axis=1, keepdims=True)
        acc = alpha * acc + _dot(p.astype(_BF16), v)
        return m_new, l, acc

    init = (jnp.full((rows, 1), NEG_BIG, _F32), jnp.zeros((rows, 1), _F32),
            jnp.zeros((rows, HEAD_DIM), _F32))
    _, l, acc = lax.fori_loop(kt_lo, kt_hi + 1, body, init)
    return acc * jnp.where(l > 0, 1.0 / l, 0.0)


def _sel_win_kernel(slopes_ref, q_ref, ks_ref, vs_ref, kw_ref, vw_ref, mask_ref, ocg_ref, gn_ref,
                    o_ref):
    h = pl.program_id(1)
    qt = pl.program_id(2)
    tq = q_ref.shape[1]
    q0 = qt * tq
    t_col = q0 + lax.broadcasted_iota(jnp.int32, (tq, 1), 0)
    sel_tk = mask_ref.shape[4]

    def sel_mask(kt, dist_i):
        return (mask_ref[0, 0, kt].astype(_F32) > 0.5) & (dist_i >= 0)

    def win_mask(kt, dist_i):
        return (dist_i >= 0) & (dist_i < WINDOW)

    win_tk = min(WIN_TK, ks_ref.shape[1])
    win_lo = jnp.maximum(q0 - (WINDOW - 1), 0) // win_tk
    gn = jax.nn.sigmoid(gn_ref[0])
    for g in range(NSA_GROUP):
        sl = slice(g * HEAD_DIM, (g + 1) * HEAD_DIM)
        qg = q_ref[0, :, sl]
        slope = slopes_ref[h * NSA_GROUP + g]
        o_slc = _flash(qg, ks_ref, vs_ref, 0, (q0 + tq - 1) // sel_tk, sel_tk, t_col, slope, sel_mask)
        o_win = _flash(qg, kw_ref, vw_ref, win_lo, (q0 + tq - 1) // win_tk, win_tk, t_col, slope, win_mask)
        o = (ocg_ref[0, :, sl] + gn[:, 3 * g + 1:3 * g + 2] * o_slc + gn[:, 3 * g + 2:3 * g + 3] * o_win)
        o_ref[0, :, sl] = o.astype(o_ref.dtype)


def _sel_win(slopes, h1, selmask, ocg, h2, seq):
    b = h1.shape[0]
    tq = min(ATT_TQ, seq)
    nkt, tk = selmask.shape[2], selmask.shape[4]
    qw = NSA_GROUP * HEAD_DIM

    def kv_spec(col):
        return pl.BlockSpec((1, seq, HEAD_DIM), lambda i, h, t: (i, 0, col // HEAD_DIM + h))

    return pl.pallas_call(
        _sel_win_kernel,
        grid=(b, NSA_KV_HEADS, seq // tq),
        in_specs=[pl.BlockSpec(memory_space=pltpu.SMEM),
                  pl.BlockSpec((1, tq, qw), lambda i, h, t: (i, t, h)),
                  kv_spec(H1_KS), kv_spec(H1_VS), kv_spec(H1_KW), kv_spec(H1_VW),
                  pl.BlockSpec((1, 1, nkt, tq, tk), lambda i, h, t: (i, h, 0, t, 0)),
                  pl.BlockSpec((1, tq, qw), lambda i, h, t: (i, t, h)),
                  pl.BlockSpec((1, tq, LANES), lambda i, h, t: (i, t, H2_GN // LANES + h))],
        out_specs=pl.BlockSpec((1, tq, qw), lambda i, h, t: (i, t, h)),
        out_shape=jax.ShapeDtypeStruct((b, seq, NSA_HEADS * HEAD_DIM), _BF16),
        compiler_params=_cparams(3, 32),
        name="nsa_sel_win",
    )(slopes, h1, h1, h1, h1, h1, selmask, ocg, h2)


def _rms_kernel(c_ref, g_ref, o_ref):
    c = c_ref[0]
    o_ref[0] = (c * lax.rsqrt(jnp.mean(c * c, axis=-1, keepdims=True) + LN_EPS) * g_ref[...]
                ).astype(o_ref.dtype)


def _rms_norm(h2, g, layer, seq):
    b = h2.shape[0]
    ts = min(512, seq)
    return pl.pallas_call(
        _rms_kernel,
        grid=(b, seq // ts),
        in_specs=[pl.BlockSpec((1, ts, KV_LATENT), lambda i, t: (i, t, H2_CKV // KV_LATENT)),
                  pl.BlockSpec((None, 1, KV_LATENT), lambda i, t: (layer, 0, 0))],
        out_specs=pl.BlockSpec((1, ts, KV_LATENT), lambda i, t: (i, t, 0)),
        out_shape=jax.ShapeDtypeStruct((b, seq, KV_LATENT), _BF16),
        compiler_params=_cparams(2, 32),
        name="dsa_rms_norm",
    )(h2, g)


def _dsa_kernel(slopes_ref, qb_ref, qi_ref, ki_ref, wi_ref, cn_ref, wuk_ref, wuv_ref, o_ref,
                key_ref, mask_ref, qlat_ref, m_ref, l_ref, acc_ref, alpha_ref, s_ref, p_ref, *, topk):
    qt = pl.program_id(1)
    tq = qb_ref.shape[1]
    seq = ki_ref.shape[1]
    tk = mask_ref.shape[2]
    q0 = qt * tq
    rows = DSA_HEADS * tq

    t_col = q0 + lax.broadcasted_iota(jnp.int32, (tq, 1), 0)
    s_row = lax.broadcasted_iota(jnp.int32, (1, seq), 1)
    causal = s_row <= t_col
    ki = ki_ref[0]
    wi = wi_ref[0]
    score = jnp.zeros((tq, seq), _F32)
    for hh in range(IDX_HEADS):
        rel = jnp.maximum(_dot_nt(qi_ref[0, :, hh * LANES:(hh + 1) * LANES], ki), 0.0)
        score = score + wi[:, hh:hh + 1] * rel
    score = jnp.where(score == 0.0, 0.0, score)
    bits = lax.bitcast_convert_type(score, jnp.int32)
    key = jnp.where(bits < 0, bits ^ jnp.int32(0x7FFFFFFF), bits)
    key_ref[...] = jnp.where(causal, key, jnp.int32(INT_MIN))

    def radix(i, thr):
        cand = thr ^ jnp.left_shift(jnp.int32(1), 31 - i)
        cnt = jnp.sum((key_ref[...] >= cand).astype(_F32), axis=1, keepdims=True)
        return jnp.where(cnt >= topk, cand, thr)

    thr = lax.fori_loop(0, 32, radix, jnp.full((tq, 1), INT_MIN, jnp.int32))

    key = key_ref[...]
    above = key > thr
    tie = key == thr
    need = topk - jnp.sum(above.astype(_F32), axis=1, keepdims=True)
    tri = (lax.broadcasted_iota(jnp.int32, (LANES, LANES), 0)
           <= lax.broadcasted_iota(jnp.int32, (LANES, LANES), 1)).astype(_BF16)
    run = jnp.zeros((tq, 1), _F32)
    per = tk // LANES
    for c in range(seq // LANES):
        sl = slice(c * LANES, (c + 1) * LANES)
        tie_c = tie[:, sl]
        prefix = _dot(tie_c.astype(_F32).astype(_BF16), tri) + run
        run = prefix[:, LANES - 1:LANES]
        chosen = (above[:, sl] | (tie_c & (prefix <= need))) & causal[:, sl]
        mask_ref[c // per, :, (c % per) * LANES:(c % per + 1) * LANES] = chosen.astype(_F32)

    for hh in range(DSA_HEADS):
        ql = _dot(qb_ref[0, :, hh * HEAD_DIM:(hh + 1) * HEAD_DIM], wuk_ref[hh]) * ATTN_SCALE
        qlat_ref[hh * tq:(hh + 1) * tq, :] = ql.astype(_BF16)

    m_ref[...] = jnp.full((rows, 1), NEG_BIG, _F32)
    l_ref[...] = jnp.zeros((rows, 1), _F32)
    acc_ref[...] = jnp.zeros((rows, KV_LATENT), _F32)

    def attend(kc, _):
        start = pl.multiple_of(kc * tk, tk)
        c = cn_ref[0, pl.ds(start, tk), :]
        dist = (t_col - (start + lax.broadcasted_iota(jnp.int32, (1, tk), 1))).astype(_F32)
        valid = mask_ref[kc] > 0.5
        s_ref[...] = _dot_nt(qlat_ref[...], c)
        for hh in range(DSA_HEADS):
            rs = slice(hh * tq, (hh + 1) * tq)
            s = jnp.where(valid, s_ref[rs, :] - slopes_ref[hh] * dist, NEG_BIG)
            m_old = m_ref[rs, :]
            m_new = jnp.maximum(m_old, jnp.max(s, axis=1, keepdims=True))
            p = jnp.where(valid, jnp.exp(s - m_new), 0.0)
            alpha = jnp.exp(m_old - m_new)
            l_ref[rs, :] = alpha * l_ref[rs, :] + jnp.sum(p, axis=1, keepdims=True)
            m_ref[rs, :] = m_new
            alpha_ref[rs, :] = alpha
            p_ref[rs, :] = p.astype(_BF16)
        acc_ref[...] = alpha_ref[...] * acc_ref[...] + _dot(p_ref[...], c)
        return 0

    lax.fori_loop(0, (q0 + tq - 1) // tk + 1, attend, 0)

    l = l_ref[...]
    o_lat = (acc_ref[...] * jnp.where(l > 0, 1.0 / l, 0.0)).astype(_BF16)
    for hh in range(DSA_HEADS):
        o = _dot(o_lat[hh * tq:(hh + 1) * tq], wuv_ref[hh])
        o_ref[0, :, hh * HEAD_DIM:(hh + 1) * HEAD_DIM] = o.astype(o_ref.dtype)


def _dsa(slopes, h1, h2, cn, wuk_t, wuv, layer, seq):
    b = h1.shape[0]
    tq = QUERY_BLOCK
    tk = min(DSA_TK, seq)
    topk = min(DSA_TOPK_MAX, seq // 4)
    width = DSA_HEADS * HEAD_DIM
    rows = DSA_HEADS * tq
    return pl.pallas_call(
        functools.partial(_dsa_kernel, topk=topk),
        grid=(b, seq // tq),
        in_specs=[pl.BlockSpec(memory_space=pltpu.SMEM),
                  pl.BlockSpec((1, tq, width), lambda i, t: (i, t, H1_Q_B // width)),
                  pl.BlockSpec((1, tq, IDX_HEADS * LANES),
                               lambda i, t: (i, t, H1_Q_IDX // (IDX_HEADS * LANES))),
                  pl.BlockSpec((1, seq, LANES), lambda i, t: (i, 0, H1_K_IDX // LANES)),
                  pl.BlockSpec((1, tq, LANES), lambda i, t: (i, t, H2_WIDX // LANES)),
                  pl.BlockSpec((1, seq, KV_LATENT), lambda i, t: (i, 0, 0)),
                  pl.BlockSpec((DSA_HEADS, HEAD_DIM, KV_LATENT), lambda i, t: (layer, 0, 0)),
                  pl.BlockSpec((DSA_HEADS, KV_LATENT, HEAD_DIM), lambda i, t: (layer, 0, 0))],
        out_specs=pl.BlockSpec((1, tq, width), lambda i, t: (i, t, 0)),
        out_shape=jax.ShapeDtypeStruct((b, seq, width), _BF16),
        scratch_shapes=[pltpu.VMEM((tq, seq), jnp.int32),
                        pltpu.VMEM((seq // tk, tq, tk), _F32),
                        pltpu.VMEM((rows, KV_LATENT), _BF16),
                        pltpu.VMEM((rows, 1), _F32),
                        pltpu.VMEM((rows, 1), _F32),
                        pltpu.VMEM((rows, KV_LATENT), _F32),
                        pltpu.VMEM((rows, 1), _F32),
                        pltpu.VMEM((rows, tk), _F32),
                        pltpu.VMEM((rows, tk), _BF16)],
        compiler_params=_cparams(2, 48),
        name="dsa_attention",
    )(slopes, h1, h1, h1, h2, cn, wuk_t, wuv)


def _gmlp_kernel(uv_ref, g_ref, b_ref, ws_ref, bs_ref, o_ref):
    z = jax.nn.gelu(uv_ref[0])
    u = z[:, :GMLP_WIDTH]
    v = z[:, GMLP_WIDTH:]
    mu = jnp.mean(v, axis=-1, keepdims=True)
    var = jnp.mean(jnp.square(v - mu), axis=-1, keepdims=True)
    vn = ((v - mu) * lax.rsqrt(var + LN_EPS) * g_ref[...] + b_ref[...]).astype(_BF16)
    t = ws_ref.shape[1]
    causal = (lax.broadcasted_iota(jnp.int32, (t, t), 0) >= lax.broadcasted_iota(jnp.int32, (t, t), 1))
    for g in range(GMLP_GROUPS):
        sl = slice(g * GMLP_GROUP_DIM, (g + 1) * GMLP_GROUP_DIM)
        w = jnp.where(causal, ws_ref[g], 0.0).astype(_BF16)
        mixed = _dot(w, vn[:, sl]) + bs_ref[:, g:g + 1]
        o_ref[0, :, sl] = (u[:, sl] * mixed).astype(o_ref.dtype)


def _gmlp(h2, ln_g, ln_b, w_s, b_s_t, layer, seq):
    b = h2.shape[0]
    t = GMLP_CHUNK
    return pl.pallas_call(
        _gmlp_kernel,
        grid=(b, seq // t),
        in_specs=[pl.BlockSpec((1, t, 2 * GMLP_WIDTH), lambda i, c: (i, c, H2_UV // (2 * GMLP_WIDTH))),
                  pl.BlockSpec((None, 1, GMLP_WIDTH), lambda i, c: (layer, 0, 0)),
                  pl.BlockSpec((None, 1, GMLP_WIDTH), lambda i, c: (layer, 0, 0)),
                  pl.BlockSpec((GMLP_GROUPS, t, t), lambda i, c: (layer, 0, 0)),
                  pl.BlockSpec((None, t, GMLP_GROUPS), lambda i, c: (layer, 0, 0))],
        out_specs=pl.BlockSpec((1, t, GMLP_WIDTH), lambda i, c: (i, c, 0)),
        out_shape=jax.ShapeDtypeStruct((b, seq, GMLP_WIDTH), _BF16),
        compiler_params=_cparams(2, 32),
        name="gmlp",
    )(h2, ln_g, ln_b, w_s, b_s_t)


def _merge_kernel(oa_ref, ob_ref, oc_ref, ga_ref, gb_ref, gc_ref, wb_ref, o_ref):
    y = ga_ref[...].astype(_F32) * _dot(oa_ref[...], wb_ref[0])
    y = y + gb_ref[...].astype(_F32) * _dot(ob_ref[...], wb_ref[1])
    y = y + gc_ref[...].astype(_F32) * _dot(oc_ref[...], wb_ref[2])
    o_ref[...] = y.astype(o_ref.dtype)


def _merge(o_a, o_b, o_c, gates, w_branch, layer):
    n_tok, width = o_a.shape
    d = w_branch.shape[2]
    tm = min(MERGE_TM, n_tok)
    tn = MERGE_TN
    nj = d // tn

    def gate_spec(n):
        return pl.BlockSpec((tm, tn), lambda i, j: (i, n * nj + j))

    branch = pl.BlockSpec((tm, width), lambda i, j: (i, 0))
    return pl.pallas_call(
        _merge_kernel,
        grid=(n_tok // tm, nj),
        in_specs=[branch, branch, branch, gate_spec(0), gate_spec(1), gate_spec(2),
                  pl.BlockSpec((N_BRANCHES, width, tn), lambda i, j: (layer, 0, j))],
        out_specs=pl.BlockSpec((tm, tn), lambda i, j: (i, j)),
        out_shape=jax.ShapeDtypeStruct((n_tok, d), _BF16),
        compiler_params=_cparams(2, 48),
        name="branch_merge",
    )(o_a, o_b, o_c, gates, gates, gates, w_branch)


def _layer_norm(y, g, b):
    mu = jnp.mean(y, axis=-1, keepdims=True)
    var = jnp.mean(jnp.square(y - mu), axis=-1, keepdims=True)
    return (y - mu) * lax.rsqrt(var + LN_EPS) * g + b


def _out_router_kernel(gs_ref, wo_ref, x_ref, g_ref, b_ref, wr_ref, br_ref, x1_ref, *, alpha):
    d = x_ref.shape[1]
    x1 = _layer_norm(alpha * x_ref[...] + _dot(gs_ref[...], wo_ref[...]), g_ref[...], b_ref[...])
    x1_ref[:, :d] = x1

    logits = jnp.dot(x1, wr_ref[...], preferred_element_type=_F32,
                     precision=lax.Precision.HIGHEST) + br_ref[...]
    gl = [logits[:, j:j + 1] for j in range(N_GROUPS)]
    gmax = functools.reduce(jnp.maximum, gl)
    gi = jnp.full_like(gmax, N_GROUPS - 1)
    for j in reversed(range(N_GROUPS - 1)):
        gi = jnp.where(gl[j] == gmax, float(j), gi)
    gp = 1.0 / functools.reduce(jnp.add, [jnp.exp(v - gmax) for v in gl])
    el = []
    for k in range(EXPERTS_PER_GROUP):
        v = jnp.zeros_like(gmax)
        for j in range(N_GROUPS):
            c = N_GROUPS + j * EXPERTS_PER_GROUP + k
            v = jnp.where(gi == float(j), logits[:, c:c + 1], v)
        el.append(v)
    e1 = functools.reduce(jnp.maximum, el)
    i1 = jnp.full_like(e1, EXPERTS_PER_GROUP - 1)
    for k in reversed(range(EXPERTS_PER_GROUP - 1)):
        i1 = jnp.where(el[k] == e1, float(k), i1)
    rest = [jnp.where(i1 == float(k), -jnp.inf, el[k]) for k in range(EXPERTS_PER_GROUP)]
    e2 = functools.reduce(jnp.maximum, rest)
    i2 = jnp.full_like(e2, EXPERTS_PER_GROUP - 1)
    for k in reversed(range(EXPERTS_PER_GROUP - 1)):
        i2 = jnp.where((rest[k] == e2) & (i1 != float(k)), float(k), i2)
    ex = jnp.exp(e2 - e1)
    w1 = gp / (1.0 + ex)
    w2 = gp * ex / (1.0 + ex)
    lane = lax.broadcasted_iota(jnp.int32, logits.shape, 1)
    rt = jnp.where(lane == 0, gi, 0.0)
    for k in range(EXPERTS_PER_GROUP):
        wk = jnp.where(i1 == float(k), w1, 0.0) + jnp.where(i2 == float(k), w2, 0.0)
        rt = jnp.where(lane == k + 1, wk, rt)
    x1_ref[:, d:] = rt


def _out_router(gs, w_out, x, ln_g, ln_b, w_r, b_r, layer, alpha):
    n_tok, d = x.shape
    tm = min(OUT_TM, n_tok)
    row = pl.BlockSpec((tm, d), lambda i: (i, 0))
    vec = pl.BlockSpec((None, 1, d), lambda i: (layer, 0, 0))
    return pl.pallas_call(
        functools.partial(_out_router_kernel, alpha=alpha),
        grid=(n_tok // tm,),
        in_specs=[row, pl.BlockSpec((None, d, d), lambda i: (layer, 0, 0)), row, vec, vec,
                  pl.BlockSpec((None, d, LANES), lambda i: (layer, 0, 0)),
                  pl.BlockSpec((None, 1, LANES), lambda i: (layer, 0, 0))],
        out_specs=pl.BlockSpec((tm, d + LANES), lambda i: (i, 0)),
        out_shape=jax.ShapeDtypeStruct((n_tok, d + LANES), _F32),
        compiler_params=_cparams(1, 48),
        name="out_proj_ln_router",
    )(gs, w_out, x, ln_g, ln_b, w_r, b_r)


def _moe_kernel(tile_group_ref, tile_rows_ref, row_token_ref, x_hbm, wg_ref, wu_ref, wd_ref,
                g_ref, b_ref, o_hbm, xg_ref, xb_ref, acc_ref, y_ref, sem, *, alpha):
    i = pl.program_id(0)
    e = pl.program_id(1)
    tm = xg_ref.shape[0]
    d = y_ref.shape[1]
    n_rows = tile_rows_ref[i]
    base = i * tm

    def row_copy_in(r):
        tok = row_token_ref[base + r]
        return pltpu.make_async_copy(x_hbm.at[pl.ds(tok, 1)], xg_ref.at[pl.ds(r, 1)], sem.at[0])

    def row_copy_out(r):
        tok = row_token_ref[base + r]
        return pltpu.make_async_copy(y_ref.at[pl.ds(r, 1)], o_hbm.at[pl.ds(tok, 1)], sem.at[1])

    @pl.when((e == 0) & (n_rows > 0))
    def _():
        def start(r, _):
            row_copy_in(r).start()
            return 0

        def wait(r, _):
            row_copy_in(r).wait()
            return 0

        lax.fori_loop(0, tm, start, 0)
        lax.fori_loop(0, tm, wait, 0)
        xb_ref[...] = xg_ref[:, :d].astype(_BF16)

    @pl.when(n_rows > 0)
    def _():
        xb = xb_ref[...]
        route = xg_ref[:, d:]
        lane = lax.broadcasted_iota(jnp.int32, route.shape, 1)
        cw = jnp.sum(jnp.where(lane == e + 1, route, 0.0), axis=1, keepdims=True)
        hid = jax.nn.silu(_dot(xb, wg_ref[0])) * _dot(xb, wu_ref[0]) * cw
        contrib = _dot(hid.astype(_BF16), wd_ref[0])

        @pl.when(e == 0)
        def _():
            acc_ref[...] = contrib

        @pl.when(e > 0)
        def _():
            acc_ref[...] += contrib

    @pl.when((e == EXPERTS_PER_GROUP - 1) & (n_rows > 0))
    def _():
        y_ref[...] = _layer_norm(alpha * xg_ref[:, :d] + acc_ref[...], g_ref[...], b_ref[...])

        def start(r, _):
            row_copy_out(r).start()
            return 0

        def wait(r, _):
            row_copy_out(r).wait()
            return 0

        lax.fori_loop(0, n_rows, start, 0)
        lax.fori_loop(0, n_rows, wait, 0)


def _moe(x1e, wg, wu, wd, ln_g, ln_b, layer, alpha):
    n_tok = x1e.shape[0]
    d = x1e.shape[1] - LANES
    tm = min(MOE_TM, n_tok)
    n_tiles = n_tok // tm + N_GROUPS
    n_rows = n_tiles * tm

    gi = x1e[:, d].astype(jnp.int32)
    onehot = (gi[:, None] == jnp.arange(N_GROUPS)[None, :]).astype(jnp.int32)
    rank = jnp.sum((jnp.cumsum(onehot, axis=0) - onehot) * onehot, axis=1)
    counts = jnp.sum(onehot, axis=0)
    padded = (counts + tm - 1) // tm * tm
    ends = jnp.cumsum(padded)
    starts = ends - padded
    dest = jnp.sum(onehot * starts[None, :], axis=1) + rank
    row_token = jnp.zeros((n_rows,), jnp.int32).at[dest].set(jnp.arange(n_tok, dtype=jnp.int32))
    tile_start = jnp.arange(n_tiles, dtype=jnp.int32) * tm
    in_group = (tile_start[:, None] >= starts[None, :]) & (tile_start[:, None] < ends[None, :])
    tile_group = jnp.sum(in_group * jnp.arange(N_GROUPS)[None, :], axis=1).astype(jnp.int32)
    tile_rows = jnp.sum(in_group * jnp.clip(starts + counts - tile_start[:, None], 0, tm), axis=1)
    tile_rows = tile_rows.astype(jnp.int32)

    def expert(i, e, tg, tr, rtok):
        return layer * N_EXPERTS + tg[i] * EXPERTS_PER_GROUP + e

    def vec(i, e, tg, tr, rtok):
        return (layer, 0, 0)

    grid_spec = pltpu.PrefetchScalarGridSpec(
        num_scalar_prefetch=3,
        grid=(n_tiles, EXPERTS_PER_GROUP),
        in_specs=[pl.BlockSpec(memory_space=pl.ANY),
                  pl.BlockSpec((1, d, D_EXPERT), lambda i, e, tg, tr, rtok: (expert(i, e, tg, tr, rtok), 0, 0)),
                  pl.BlockSpec((1, d, D_EXPERT), lambda i, e, tg, tr, rtok: (expert(i, e, tg, tr, rtok), 0, 0)),
                  pl.BlockSpec((1, D_EXPERT, d), lambda i, e, tg, tr, rtok: (expert(i, e, tg, tr, rtok), 0, 0)),
                  pl.BlockSpec((None, 1, d), vec),
                  pl.BlockSpec((None, 1, d), vec)],
        out_specs=pl.BlockSpec(memory_space=pl.ANY),
        scratch_shapes=[pltpu.VMEM((tm, d + LANES), _F32),
                        pltpu.VMEM((tm, d), _BF16),
                        pltpu.VMEM((tm, d), _F32),
                        pltpu.VMEM((tm, d), _F32),
                        pltpu.SemaphoreType.DMA((2,))],
    )
    return pl.pallas_call(
        functools.partial(_moe_kernel, alpha=alpha),
        grid_spec=grid_spec,
        out_shape=jax.ShapeDtypeStruct((n_tok, d), _F32),
        compiler_params=_cparams(2, 48),
        name="moe_ln2",
    )(tile_group, tile_rows, row_token, x1e, wg, wu, wd, ln_g, ln_b)


def _in_offsets(d):
    names = ("q_a", "k_c", "v_c", "k_s", "v_s", "k_w", "v_w", "g_nsa", "q_b", "c_kv", "q_idx", "k_idx",
             "w_idx", "uv", "g_merge")
    kv = NSA_KV_HEADS * HEAD_DIM
    sizes = (NSA_HEADS * HEAD_DIM, kv, kv, kv, kv, kv, kv, NSA_HEADS * 3, DSA_HEADS * HEAD_DIM, KV_LATENT,
             IDX_HEADS * IDX_DIM, IDX_DIM, IDX_HEADS, 2 * GMLP_WIDTH, N_BRANCHES * d)
    offs, start = {}, 0
    for name, n in zip(names, sizes):
        offs[name] = start
        start += n
    return offs, start


def _pack_kernel(w_ref, o1_ref, o2_ref, o4_ref, *, offs):
    rows = w_ref.shape[1]

    def put(dst, dst_off, src_off, n, slot=None):
        slot = n if slot is None else slot
        piece = w_ref[0, :, src_off:src_off + n].astype(_BF16)
        if slot > n:
            piece = jnp.concatenate([piece, jnp.zeros((rows, slot - n), _BF16)], axis=1)
        dst[0, :, dst_off:dst_off + slot] = piece

    kv = NSA_KV_HEADS * HEAD_DIM
    put(o1_ref, H1_Q_A, offs["q_a"], NSA_HEADS * HEAD_DIM)
    put(o1_ref, H1_KS, offs["k_s"], 4 * kv)
    put(o1_ref, H1_Q_B, offs["q_b"], DSA_HEADS * HEAD_DIM)
    for h in range(IDX_HEADS):
        put(o1_ref, H1_Q_IDX + h * LANES, offs["q_idx"] + h * IDX_DIM, IDX_DIM, LANES)
    put(o1_ref, H1_K_IDX, offs["k_idx"], IDX_DIM, H1_COLS - H1_K_IDX)

    put(o2_ref, H2_UV, offs["uv"], 2 * GMLP_WIDTH)
    put(o2_ref, H2_KC, offs["k_c"], 2 * kv)
    put(o2_ref, H2_CKV, offs["c_kv"], KV_LATENT)
    per_head = NSA_GROUP * 3
    for h in range(NSA_KV_HEADS):
        put(o2_ref, H2_GN + h * LANES, offs["g_nsa"] + h * per_head, per_head, LANES)
    put(o2_ref, H2_WIDX, offs["w_idx"], IDX_HEADS, LANES)

    put(o4_ref, 0, offs["g_merge"], o4_ref.shape[2])


def _pack_w_in(w_in):
    layers, d, cols = w_in.shape
    offs, total = _in_offsets(d)
    assert total == cols
    tr = 128
    return pl.pallas_call(
        functools.partial(_pack_kernel, offs=offs),
        grid=(layers, d // tr),
        in_specs=[pl.BlockSpec((1, tr, cols), lambda l, r: (l, r, 0))],
        out_specs=[pl.BlockSpec((1, tr, H1_COLS), lambda l, r: (l, r, 0)),
                   pl.BlockSpec((1, tr, H2_COLS), lambda l, r: (l, r, 0)),
                   pl.BlockSpec((1, tr, N_BRANCHES * d), lambda l, r: (l, r, 0))],
        out_shape=[jax.ShapeDtypeStruct((layers, d, H1_COLS), _BF16),
                   jax.ShapeDtypeStruct((layers, d, H2_COLS), _BF16),
                   jax.ShapeDtypeStruct((layers, d, N_BRANCHES * d), _BF16)],
        compiler_params=_cparams(2, 48),
        name="pack_w_in",
    )(w_in)


def kernel(x, w_in, cmp_w1, cmp_w2, cmp_pe, w_uk, w_uv, kv_norm_g, gmlp_ln_g, gmlp_ln_b, gmlp_w_s, gmlp_b_s, w_branch, w_out, ln1_g, ln1_b, router_group_w, router_group_b, router_expert_w, router_expert_b, expert_w_gate, expert_w_up, expert_w_down, ln2_g, ln2_b):
    b, seq, d = x.shape
    depth = w_in.shape[0]
    alpha = float((2 * depth) ** 0.25)
    n_tok = b * seq
    slopes = 2.0 ** (-8.0 * jnp.arange(1, NSA_HEADS + 1, dtype=_F32) / NSA_HEADS)

    w1, w2, w4 = _pack_w_in(w_in)
    pe = cmp_pe.reshape(depth * 2, CMP_BLOCK, HEAD_DIM)
    cw1 = cmp_w1.astype(_BF16).reshape(depth * 2, CMP_BLOCK * HEAD_DIM, HEAD_DIM)
    cw2 = cmp_w2.astype(_BF16).reshape(depth * 2, HEAD_DIM, HEAD_DIM)
    wuk_t = w_uk.transpose(0, 1, 3, 2).astype(_BF16).reshape(depth * DSA_HEADS, HEAD_DIM, KV_LATENT)
    wuv = w_uv.astype(_BF16).reshape(depth * DSA_HEADS, KV_LATENT, HEAD_DIM)
    kvg = kv_norm_g.reshape(depth, 1, KV_LATENT)
    g_ln_g = gmlp_ln_g.reshape(depth, 1, GMLP_WIDTH)
    g_ln_b = gmlp_ln_b.reshape(depth, 1, GMLP_WIDTH)
    g_ws = gmlp_w_s.reshape(depth * GMLP_GROUPS, GMLP_CHUNK, GMLP_CHUNK)
    g_bs_t = gmlp_b_s.transpose(0, 2, 1)
    wb = w_branch.astype(_BF16).reshape(depth * N_BRANCHES, w_branch.shape[2], d)
    wo = w_out.astype(_BF16)
    n_route = N_GROUPS + N_EXPERTS
    w_r = jnp.pad(jnp.concatenate([router_group_w, router_expert_w], axis=2),
                  ((0, 0), (0, 0), (0, LANES - n_route)))
    b_r = jnp.pad(jnp.concatenate([router_group_b, router_expert_b], axis=1),
                  ((0, 0), (0, LANES - n_route))).reshape(depth, 1, LANES)
    wg = expert_w_gate.astype(_BF16).reshape(depth * N_EXPERTS, d, D_EXPERT)
    wu = expert_w_up.astype(_BF16).reshape(depth * N_EXPERTS, d, D_EXPERT)
    wd = expert_w_down.astype(_BF16).reshape(depth * N_EXPERTS, D_EXPERT, d)
    l1g, l1b = ln1_g.reshape(depth, 1, d), ln1_b.reshape(depth, 1, d)
    l2g, l2b = ln2_g.reshape(depth, 1, d), ln2_b.reshape(depth, 1, d)

    xt = x.reshape(n_tok, d)
    for l in range(depth):
        h1 = _matmul(xt, w1, l, _BF16, 768).reshape(b, seq, H1_COLS)
        h2 = _matmul(xt, w2, l, _F32, 640).reshape(b, seq, H2_COLS)
        gates = _matmul(xt, w4, l, _BF16, 768, act="sigmoid")

        kcv = _compress(h2, pe, cw1, cw2, l, seq)
        ocg, selmask = _cmp_select(slopes, h1, kcv, h2, seq)
        o_a = _sel_win(slopes, h1, selmask, ocg, h2, seq)

        cn = _rms_norm(h2, kvg, l, seq)
        o_b = _dsa(slopes, h1, h2, cn, wuk_t, wuv, l, seq)

        o_c = _gmlp(h2, g_ln_g, g_ln_b, g_ws, g_bs_t, l, seq)

        gs = _merge(o_a.reshape(n_tok, -1), o_b.reshape(n_tok, -1), o_c.reshape(n_tok, -1), gates, wb, l)
        x1e = _out_router(gs, wo, xt, l1g, l1b, w_r, b_r, l, alpha)
        xt = _moe(x1e, wg, wu, wd, l2g, l2b, l, alpha)
    return xt.reshape(b, seq, d)
```

```python
import functools

import jax
import jax.numpy as jnp
from jax import lax
from jax.experimental import pallas as pl
from jax.experimental.pallas import tpu as pltpu

HEAD_DIM = 128
NSA_HEADS = 8
NSA_KV_HEADS = 2
NSA_GROUP = NSA_HEADS // NSA_KV_HEADS
CMP_BLOCK = 32
CMP_STRIDE = 16
SEL_BLOCK = 64
N_SEL = 16
WINDOW = 512
FORCE_BONUS = 1e4
DSA_HEADS = 8
KV_LATENT = 256
IDX_HEADS = 4
IDX_DIM = 64
DSA_TOPK_MAX = 256
QUERY_BLOCK = 128
GMLP_GROUPS = 8
GMLP_GROUP_DIM = 128
GMLP_WIDTH = GMLP_GROUPS * GMLP_GROUP_DIM
GMLP_CHUNK = 128
N_BRANCHES = 3
N_GROUPS = 4
EXPERTS_PER_GROUP = 4
N_EXPERTS = N_GROUPS * EXPERTS_PER_GROUP
D_EXPERT = 512
LN_EPS = 1e-5
NEG_BIG = -1e30
ATTN_SCALE = HEAD_DIM ** -0.5

LANES = 128
INT_MIN = -2 ** 31
MIB = 1024 * 1024

H1_Q_A = 0
H1_KS, H1_VS, H1_KW, H1_VW = 1024, 1280, 1536, 1792
H1_Q_B = 2048
H1_Q_IDX = 3072
H1_K_IDX = 3584
H1_COLS = 3840
H2_UV = 0
H2_KC, H2_VC = 2048, 2304
H2_CKV = 2560
H2_GN = 2816
H2_WIDX = 3072
H2_COLS = 3200

MM_TM = 1024
CMP_TQ = 256
SEL_TK = 256
ATT_TQ = 128
WIN_TK = 128
DSA_TK = 512
MERGE_TM = 512
MERGE_TN = 512
OUT_TM = 512
MOE_TM = 512

_F32 = jnp.float32
_BF16 = jnp.bfloat16


def _cparams(n_axes, vmem_mib):
    return pltpu.CompilerParams(dimension_semantics=("arbitrary",) * n_axes,
                                vmem_limit_bytes=vmem_mib * MIB)


def _dot(a, b):
    return jnp.dot(a, b, preferred_element_type=_F32)


def _dot_nt(a, b):
    return lax.dot_general(a, b, (((1,), (1,)), ((), ())), preferred_element_type=_F32)


def _mm_kernel(a_ref, b_ref, o_ref, a_bf_ref, *, act):
    @pl.when(pl.program_id(1) == 0)
    def _():
        a_bf_ref[...] = a_ref[...].astype(_BF16)

    acc = _dot(a_bf_ref[...], b_ref[...])
    if act == "sigmoid":
        acc = jax.nn.sigmoid(acc)
    o_ref[...] = acc.astype(o_ref.dtype)


def _matmul(a, w, layer, out_dtype, tn, act=None):
    m, k = a.shape
    n = w.shape[2]
    tm = min(MM_TM, m)
    return pl.pallas_call(
        functools.partial(_mm_kernel, act=act),
        grid=(m // tm, n // tn),
        in_specs=[pl.BlockSpec((tm, k), lambda i, j: (i, 0)),
                  pl.BlockSpec((None, k, tn), lambda i, j: (layer, 0, j))],
        out_specs=pl.BlockSpec((tm, tn), lambda i, j: (i, j)),
        out_shape=jax.ShapeDtypeStruct((m, n), out_dtype),
        scratch_shapes=[pltpu.VMEM((tm, k), _BF16)],
        compiler_params=_cparams(2, 48),
        name="proj_matmul",
    )(a, w)


def _compress_kernel(x_ref, pe_ref, w1_ref, w2_ref, o_ref):
    nc = o_ref.shape[3]
    top = jnp.zeros((nc, HEAD_DIM), _F32)
    bot = jnp.zeros((nc, HEAD_DIM), _F32)
    for ll in range(CMP_STRIDE):
        y = x_ref[0, pl.ds(ll, nc, stride=CMP_STRIDE), :]
        lo, hi = ll, CMP_STRIDE + ll
        top = top + _dot((y + pe_ref[0, lo:lo + 1, :]).astype(_BF16),
                         w1_ref[0, lo * HEAD_DIM:(lo + 1) * HEAD_DIM, :])
        bot = bot + _dot((y + pe_ref[0, hi:hi + 1, :]).astype(_BF16),
                         w1_ref[0, hi * HEAD_DIM:(hi + 1) * HEAD_DIM, :])
    pre = top + pltpu.roll(bot, nc - 1, 0)
    o_ref[0, 0, 0] = _dot(jax.nn.gelu(pre).astype(_BF16), w2_ref[0]).astype(o_ref.dtype)


def _compress(h2, pe, w1, w2, layer, seq):
    b = h2.shape[0]
    nc = seq // CMP_STRIDE
    col = H2_KC // HEAD_DIM
    return pl.pallas_call(
        _compress_kernel,
        grid=(b, 2, NSA_KV_HEADS),
        in_specs=[pl.BlockSpec((1, seq, HEAD_DIM), lambda i, j, h: (i, 0, col + j * NSA_KV_HEADS + h)),
                  pl.BlockSpec((1, CMP_BLOCK, HEAD_DIM), lambda i, j, h: (2 * layer + j, 0, 0)),
                  pl.BlockSpec((1, CMP_BLOCK * HEAD_DIM, HEAD_DIM), lambda i, j, h: (2 * layer + j, 0, 0)),
                  pl.BlockSpec((1, HEAD_DIM, HEAD_DIM), lambda i, j, h: (2 * layer + j, 0, 0))],
        out_specs=pl.BlockSpec((1, 1, 1, nc, HEAD_DIM), lambda i, j, h: (i, j, h, 0, 0)),
        out_shape=jax.ShapeDtypeStruct((b, 2, NSA_KV_HEADS, nc, HEAD_DIM), _BF16),
        compiler_params=_cparams(3, 32),
        name="nsa_compress",
    )(h2, pe, w1, w2)


def _cmp_select_kernel(slopes_ref, q_ref, kc_ref, vc_ref, gn_ref, ocg_ref, mask_ref, *, seq):
    h = pl.program_id(1)
    qt = pl.program_id(2)
    tq = q_ref.shape[1]
    nc = kc_ref.shape[3]
    n_cmp = (seq - CMP_BLOCK) // CMP_STRIDE + 1
    n_slc = seq // SEL_BLOCK

    t_col = qt * tq + lax.broadcasted_iota(jnp.int32, (tq, 1), 0)
    i_row = lax.broadcasted_iota(jnp.int32, (1, nc), 1)
    dist_i = t_col - (i_row * CMP_STRIDE + (CMP_BLOCK - 1))
    valid = (dist_i >= 0) & (i_row < n_cmp)
    dist = dist_i.astype(_F32)

    kc = kc_ref[0, 0, 0]
    vc = vc_ref[0, 0, 0]
    gn = jax.nn.sigmoid(gn_ref[0])
    psum = jnp.zeros((tq, nc), _F32)
    for g in range(NSA_GROUP):
        qg = q_ref[0, :, g * HEAD_DIM:(g + 1) * HEAD_DIM]
        slope = slopes_ref[h * NSA_GROUP + g]
        s = _dot_nt(qg, kc) * ATTN_SCALE - slope * dist
        s = jnp.where(valid, s, NEG_BIG)
        e = jnp.exp(s - jnp.max(s, axis=1, keepdims=True))
        p = e / jnp.sum(e, axis=1, keepdims=True)
        p = jnp.where(valid, p, 0.0)
        o = _dot(p.astype(_BF16), vc)
        ocg_ref[0, :, g * HEAD_DIM:(g + 1) * HEAD_DIM] = gn[:, 3 * g:3 * g + 1] * o
        psum = psum + p

    ci = lax.broadcasted_iota(jnp.int32, (nc, LANES), 0) * CMP_STRIDE
    sj = lax.broadcasted_iota(jnp.int32, (nc, LANES), 1) * SEL_BLOCK
    overlap = ((ci < sj + SEL_BLOCK) & (ci + CMP_BLOCK > sj)).astype(_F32)
    imp = jnp.dot(psum, overlap, preferred_element_type=_F32, precision=lax.Precision.HIGHEST)

    j_row = lax.broadcasted_iota(jnp.int32, (1, LANES), 1)
    cur = t_col >> (SEL_BLOCK.bit_length() - 1)
    forced = (j_row == 0) | (j_row == cur) | (j_row == cur - 1)
    imp = jnp.where(j_row <= cur, imp + jnp.where(forced, FORCE_BONUS, 0.0), NEG_BIG)
    imp = jnp.where(j_row < n_slc, imp, -3e38)
    rank = jnp.zeros((tq, LANES), _F32)
    for jp in range(n_slc):
        col = imp[:, jp:jp + 1]
        before = (col > imp) | ((col == imp) & (j_row > jp))
        rank = rank + before.astype(_F32)
    sel = (rank < min(N_SEL, n_slc)).astype(_BF16)

    tk = mask_ref.shape[4]
    bj = lax.broadcasted_iota(jnp.int32, (LANES, tk), 0)
    ks = lax.broadcasted_iota(jnp.int32, (LANES, tk), 1)
    for kt in range(mask_ref.shape[2]):
        expand = (bj == ((ks + kt * tk) >> (SEL_BLOCK.bit_length() - 1))).astype(_BF16)
        mask_ref[0, 0, kt] = _dot(sel, expand).astype(mask_ref.dtype)


def _cmp_select(slopes, h1, kcv, h2, seq):
    b = h1.shape[0]
    nc = kcv.shape[3]
    tq = min(CMP_TQ, seq)
    tk = min(SEL_TK, seq)
    nkt = seq // tk
    qw = NSA_GROUP * HEAD_DIM
    return pl.pallas_call(
        functools.partial(_cmp_select_kernel, seq=seq),
        grid=(b, NSA_KV_HEADS, seq // tq),
        in_specs=[pl.BlockSpec(memory_space=pltpu.SMEM),
                  pl.BlockSpec((1, tq, qw), lambda i, h, t: (i, t, h)),
                  pl.BlockSpec((1, 1, 1, nc, HEAD_DIM), lambda i, h, t: (i, 0, h, 0, 0)),
                  pl.BlockSpec((1, 1, 1, nc, HEAD_DIM), lambda i, h, t: (i, 1, h, 0, 0)),
                  pl.BlockSpec((1, tq, LANES), lambda i, h, t: (i, t, H2_GN // LANES + h))],
        out_specs=[pl.BlockSpec((1, tq, qw), lambda i, h, t: (i, t, h)),
                   pl.BlockSpec((1, 1, nkt, tq, tk), lambda i, h, t: (i, h, 0, t, 0))],
        out_shape=[jax.ShapeDtypeStruct((b, seq, NSA_HEADS * HEAD_DIM), _F32),
                   jax.ShapeDtypeStruct((b, NSA_KV_HEADS, nkt, seq, tk), _BF16)],
        compiler_params=_cparams(3, 32),
        name="nsa_cmp_select",
    )(slopes, h1, kcv, kcv, h2)


def _flash(q4, k_ref, v_ref, kt_lo, kt_hi, tk, t4, slope4, mask_fn):
    rows = q4.shape[0]
    rel = t4 - lax.broadcasted_iota(jnp.int32, (1, tk), 1)
    base = slope4 * rel.astype(_F32)

    def body(kt, carry):
        m, l, acc = carry
        start = pl.multiple_of(kt * tk, tk)
        k = k_ref[0, pl.ds(start, tk), :]
        v = v_ref[0, pl.ds(start, tk), :]
        valid = mask_fn(kt, rel, start)
        s = _dot_nt(q4, k) * ATTN_SCALE - (base - slope4 * start.astype(_F32))
        s = jnp.where(valid, s, NEG_BIG)
        m_new = jnp.maximum(m, jnp.max(s, axis=1, keepdims=True))
        p = jnp.where(valid, jnp.exp(s - m_new), 0.0)
        alpha = jnp.exp(m - m_new)
        l = alpha * l + jnp.sum(p, axis=1, keepdims=True)
        acc = alpha * acc + _dot(p.astype(_BF16), v)
        return m_new, l, acc

    init = (jnp.full((rows, 1), NEG_BIG, _F32), jnp.zeros((rows, 1), _F32),
            jnp.zeros((rows, HEAD_DIM), _F32))
    _, l, acc = lax.fori_loop(kt_lo, kt_hi + 1, body, init)
    return acc * jnp.where(l > 0, 1.0 / l, 0.0)


def _sel_win_kernel(slopes_ref, q_ref, ks_ref, vs_ref, kw_ref, vw_ref, mask_ref, ocg_ref, gn_ref,
                    o_ref):
    h = pl.program_id(1)
    qt = pl.program_id(2)
    tq = q_ref.shape[1]
    rows = NSA_GROUP * tq
    q0 = qt * tq
    q4 = jnp.concatenate([q_ref[0, :, g * HEAD_DIM:(g + 1) * HEAD_DIM] for g in range(NSA_GROUP)],
                         axis=0)
    row = lax.broadcasted_iota(jnp.int32, (rows, 1), 0)
    t4 = q0 + (row & (tq - 1))
    g4 = row >> (tq.bit_length() - 1)
    slope4 = jnp.zeros((rows, 1), _F32)
    for g in range(NSA_GROUP):
        slope4 = jnp.where(g4 == g, slopes_ref[h * NSA_GROUP + g], slope4)

    sel_tk = mask_ref.shape[4]

    def sel_mask(kt, rel, start):
        mk = mask_ref[0, 0, kt].astype(_F32)
        mk4 = jnp.concatenate([mk] * NSA_GROUP, axis=0)
        return (mk4 > 0.5) & (rel >= start)

    def win_mask(kt, rel, start):
        return (rel >= start) & (rel < start + WINDOW)

    o_slc = _flash(q4, ks_ref, vs_ref, 0, (q0 + tq - 1) // sel_tk, sel_tk, t4, slope4, sel_mask)
    win_tk = min(WIN_TK, ks_ref.shape[1])
    win_lo = jnp.maximum(q0 - (WINDOW - 1), 0) // win_tk
    o_win = _flash(q4, kw_ref, vw_ref, win_lo, (q0 + tq - 1) // win_tk, win_tk, t4, slope4, win_mask)

    gn = jax.nn.sigmoid(gn_ref[0])
    for g in range(NSA_GROUP):
        o = (ocg_ref[0, :, g * HEAD_DIM:(g + 1) * HEAD_DIM]
             + gn[:, 3 * g + 1:3 * g + 2] * o_slc[g * tq:(g + 1) * tq]
             + gn[:, 3 * g + 2:3 * g + 3] * o_win[g * tq:(g + 1) * tq])
        o_ref[0, :, g * HEAD_DIM:(g + 1) * HEAD_DIM] = o.astype(o_ref.dtype)


def _sel_win(slopes, h1, selmask, ocg, h2, seq):
    b = h1.shape[0]
    tq = min(ATT_TQ, seq)
    nkt, tk = selmask.shape[2], selmask.shape[4]
    qw = NSA_GROUP * HEAD_DIM

    def kv_spec(col):
        return pl.BlockSpec((1, seq, HEAD_DIM), lambda i, h, t: (i, 0, col // HEAD_DIM + h))

    return pl.pallas_call(
        _sel_win_kernel,
        grid=(b, NSA_KV_HEADS, seq // tq),
        in_specs=[pl.BlockSpec(memory_space=pltpu.SMEM),
                  pl.BlockSpec((1, tq, qw), lambda i, h, t: (i, t, h)),
                  kv_spec(H1_KS), kv_spec(H1_VS), kv_spec(H1_KW), kv_spec(H1_VW),
                  pl.BlockSpec((1, 1, nkt, tq, tk), lambda i, h, t: (i, h, 0, t, 0)),
                  pl.BlockSpec((1, tq, qw), lambda i, h, t: (i, t, h)),
                  pl.BlockSpec((1, tq, LANES), lambda i, h, t: (i, t, H2_GN // LANES + h))],
        out_specs=pl.BlockSpec((1, tq, qw), lambda i, h, t: (i, t, h)),
        out_shape=jax.ShapeDtypeStruct((b, seq, NSA_HEADS * HEAD_DIM), _BF16),
        compiler_params=_cparams(3, 32),
        name="nsa_sel_win",
    )(slopes, h1, h1, h1, h1, h1, selmask, ocg, h2)


def _rms_kernel(c_ref, g_ref, o_ref):
    c = c_ref[0]
    o_ref[0] = (c * lax.rsqrt(jnp.mean(c * c, axis=-1, keepdims=True) + LN_EPS) * g_ref[...]
                ).astype(o_ref.dtype)


def _rms_norm(h2, g, layer, seq):
    b = h2.shape[0]
    ts = min(512, seq)
    return pl.pallas_call(
        _rms_kernel,
        grid=(b, seq // ts),
        in_specs=[pl.BlockSpec((1, ts, KV_LATENT), lambda i, t: (i, t, H2_CKV // KV_LATENT)),
                  pl.BlockSpec((None, 1, KV_LATENT), lambda i, t: (layer, 0, 0))],
        out_specs=pl.BlockSpec((1, ts, KV_LATENT), lambda i, t: (i, t, 0)),
        out_shape=jax.ShapeDtypeStruct((b, seq, KV_LATENT), _BF16),
        compiler_params=_cparams(2, 32),
        name="dsa_rms_norm",
    )(h2, g)


def _dsa_kernel(slopes_ref, qb_ref, qi_ref, ki_ref, wi_ref, cn_ref, wuk_ref, wuv_ref, o_ref,
                key_ref, mask_ref, qlat_ref, m_ref, l_ref, acc_ref, *, topk):
    qt = pl.program_id(1)
    tq = qb_ref.shape[1]
    seq = ki_ref.shape[1]
    tk = mask_ref.shape[2]
    q0 = qt * tq
    rows = DSA_HEADS * tq

    t_col = q0 + lax.broadcasted_iota(jnp.int32, (tq, 1), 0)
    s_row = lax.broadcasted_iota(jnp.int32, (1, seq), 1)
    causal = s_row <= t_col
    ki = ki_ref[0]
    wi = wi_ref[0]
    score = jnp.zeros((tq, seq), _F32)
    for hh in range(IDX_HEADS):
        rel = jnp.maximum(_dot_nt(qi_ref[0, :, hh * LANES:(hh + 1) * LANES], ki), 0.0)
        score = score + wi[:, hh:hh + 1] * rel
    score = jnp.where(score == 0.0, 0.0, score)
    bits = lax.bitcast_convert_type(score, jnp.int32)
    key = jnp.where(bits < 0, bits ^ jnp.int32(0x7FFFFFFF), bits)
    key_ref[...] = jnp.where(causal, key, jnp.int32(INT_MIN))

    def radix(i, thr):
        cand = thr ^ jnp.left_shift(jnp.int32(1), 31 - i)
        cnt = jnp.sum((key_ref[...] >= cand).astype(_F32), axis=1, keepdims=True)
        return jnp.where(cnt >= topk, cand, thr)

    thr = lax.fori_loop(0, 32, radix, jnp.full((tq, 1), INT_MIN, jnp.int32))

    key = key_ref[...]
    above = key > thr
    tie = key == thr
    need = topk - jnp.sum(above.astype(_F32), axis=1, keepdims=True)
    tri = (lax.broadcasted_iota(jnp.int32, (LANES, LANES), 0)
           <= lax.broadcasted_iota(jnp.int32, (LANES, LANES), 1)).astype(_BF16)
    run = jnp.zeros((tq, 1), _F32)
    per = tk // LANES
    for c in range(seq // LANES):
        sl = slice(c * LANES, (c + 1) * LANES)
        tie_c = tie[:, sl]
        prefix = _dot(tie_c.astype(_F32).astype(_BF16), tri) + run
        run = prefix[:, LANES - 1:LANES]
        chosen = (above[:, sl] | (tie_c & (prefix <= need))) & causal[:, sl]
        mask_ref[c // per, :, (c % per) * LANES:(c % per + 1) * LANES] = chosen.astype(_F32)

    for hh in range(DSA_HEADS):
        ql = _dot(qb_ref[0, :, hh * HEAD_DIM:(hh + 1) * HEAD_DIM], wuk_ref[hh]) * ATTN_SCALE
        qlat_ref[hh * tq:(hh + 1) * tq, :] = ql.astype(_BF16)

    row = lax.broadcasted_iota(jnp.int32, (rows, 1), 0)
    t8 = q0 + (row & (tq - 1))
    h8 = row >> (tq.bit_length() - 1)
    slope8 = jnp.zeros((rows, 1), _F32)
    for hh in range(DSA_HEADS):
        slope8 = jnp.where(h8 == hh, slopes_ref[hh], slope8)

    m_ref[...] = jnp.full((rows, 1), NEG_BIG, _F32)
    l_ref[...] = jnp.zeros((rows, 1), _F32)
    acc_ref[...] = jnp.zeros((rows, KV_LATENT), _F32)
    base = slope8 * (t8 - lax.broadcasted_iota(jnp.int32, (1, tk), 1)).astype(_F32)

    def attend(kc, _):
        start = pl.multiple_of(kc * tk, tk)
        c = cn_ref[0, pl.ds(start, tk), :]
        mk = mask_ref[kc]
        valid = jnp.concatenate([mk] * DSA_HEADS, axis=0) > 0.5
        s = _dot_nt(qlat_ref[...], c) - (base - slope8 * start.astype(_F32))
        s = jnp.where(valid, s, NEG_BIG)
        m_old = m_ref[...]
        m_new = jnp.maximum(m_old, jnp.max(s, axis=1, keepdims=True))
        p = jnp.where(valid, jnp.exp(s - m_new), 0.0)
        alpha = jnp.exp(m_old - m_new)
        l_ref[...] = alpha * l_ref[...] + jnp.sum(p, axis=1, keepdims=True)
        acc_ref[...] = alpha * acc_ref[...] + _dot(p.astype(_BF16), c)
        m_ref[...] = m_new
        return 0

    lax.fori_loop(0, (q0 + tq - 1) // tk + 1, attend, 0)

    l = l_ref[...]
    o_lat = (acc_ref[...] * jnp.where(l > 0, 1.0 / l, 0.0)).astype(_BF16)
    for hh in range(DSA_HEADS):
        o = _dot(o_lat[hh * tq:(hh + 1) * tq], wuv_ref[hh])
        o_ref[0, :, hh * HEAD_DIM:(hh + 1) * HEAD_DIM] = o.astype(o_ref.dtype)


def _dsa(slopes, h1, h2, cn, wuk_t, wuv, layer, seq):
    b = h1.shape[0]
    tq = QUERY_BLOCK
    tk = min(DSA_TK, seq)
    topk = min(DSA_TOPK_MAX, seq // 4)
    width = DSA_HEADS * HEAD_DIM
    rows = DSA_HEADS * tq
    return pl.pallas_call(
        functools.partial(_dsa_kernel, topk=topk),
        grid=(b, seq // tq),
        in_specs=[pl.BlockSpec(memory_space=pltpu.SMEM),
                  pl.BlockSpec((1, tq, width), lambda i, t: (i, t, H1_Q_B // width)),
                  pl.BlockSpec((1, tq, IDX_HEADS * LANES),
                               lambda i, t: (i, t, H1_Q_IDX // (IDX_HEADS * LANES))),
                  pl.BlockSpec((1, seq, LANES), lambda i, t: (i, 0, H1_K_IDX // LANES)),
                  pl.BlockSpec((1, tq, LANES), lambda i, t: (i, t, H2_WIDX // LANES)),
                  pl.BlockSpec((1, seq, KV_LATENT), lambda i, t: (i, 0, 0)),
                  pl.BlockSpec((DSA_HEADS, HEAD_DIM, KV_LATENT), lambda i, t: (layer, 0, 0)),
                  pl.BlockSpec((DSA_HEADS, KV_LATENT, HEAD_DIM), lambda i, t: (layer, 0, 0))],
        out_specs=pl.BlockSpec((1, tq, width), lambda i, t: (i, t, 0)),
        out_shape=jax.ShapeDtypeStruct((b, seq, width), _BF16),
        scratch_shapes=[pltpu.VMEM((tq, seq), jnp.int32),
                        pltpu.VMEM((seq // tk, tq, tk), _F32),
                        pltpu.VMEM((rows, KV_LATENT), _BF16),
                        pltpu.VMEM((rows, 1), _F32),
                        pltpu.VMEM((rows, 1), _F32),
                        pltpu.VMEM((rows, KV_LATENT), _F32)],
        compiler_params=_cparams(2, 48),
        name="dsa_attention",
    )(slopes, h1, h1, h1, h2, cn, wuk_t, wuv)


def _gmlp_kernel(uv_ref, g_ref, b_ref, ws_ref, bs_ref, o_ref):
    z = jax.nn.gelu(uv_ref[0])
    u = z[:, :GMLP_WIDTH]
    v = z[:, GMLP_WIDTH:]
    mu = jnp.mean(v, axis=-1, keepdims=True)
    var = jnp.mean(jnp.square(v - mu), axis=-1, keepdims=True)
    vn = ((v - mu) * lax.rsqrt(var + LN_EPS) * g_ref[...] + b_ref[...]).astype(_BF16)
    t = ws_ref.shape[1]
    causal = (lax.broadcasted_iota(jnp.int32, (t, t), 0) >= lax.broadcasted_iota(jnp.int32, (t, t), 1))
    for g in range(GMLP_GROUPS):
        sl = slice(g * GMLP_GROUP_DIM, (g + 1) * GMLP_GROUP_DIM)
        w = jnp.where(causal, ws_ref[g], 0.0).astype(_BF16)
        mixed = _dot(w, vn[:, sl]) + bs_ref[:, g:g + 1]
        o_ref[0, :, sl] = (u[:, sl] * mixed).astype(o_ref.dtype)


def _gmlp(h2, ln_g, ln_b, w_s, b_s_t, layer, seq):
    b = h2.shape[0]
    t = GMLP_CHUNK
    return pl.pallas_call(
        _gmlp_kernel,
        grid=(b, seq // t),
        in_specs=[pl.BlockSpec((1, t, 2 * GMLP_WIDTH), lambda i, c: (i, c, H2_UV // (2 * GMLP_WIDTH))),
                  pl.BlockSpec((None, 1, GMLP_WIDTH), lambda i, c: (layer, 0, 0)),
                  pl.BlockSpec((None, 1, GMLP_WIDTH), lambda i, c: (layer, 0, 0)),
                  pl.BlockSpec((GMLP_GROUPS, t, t), lambda i, c: (layer, 0, 0)),
                  pl.BlockSpec((None, t, GMLP_GROUPS), lambda i, c: (layer, 0, 0))],
        out_specs=pl.BlockSpec((1, t, GMLP_WIDTH), lambda i, c: (i, c, 0)),
        out_shape=jax.ShapeDtypeStruct((b, seq, GMLP_WIDTH), _BF16),
        compiler_params=_cparams(2, 32),
        name="gmlp",
    )(h2, ln_g, ln_b, w_s, b_s_t)


def _merge_kernel(oa_ref, ob_ref, oc_ref, ga_ref, gb_ref, gc_ref, wb_ref, o_ref):
    y = ga_ref[...].astype(_F32) * _dot(oa_ref[...], wb_ref[0])
    y = y + gb_ref[...].astype(_F32) * _dot(ob_ref[...], wb_ref[1])
    y = y + gc_ref[...].astype(_F32) * _dot(oc_ref[...], wb_ref[2])
    o_ref[...] = y.astype(o_ref.dtype)


def _merge(o_a, o_b, o_c, gates, w_branch, layer):
    n_tok, width = o_a.shape
    d = w_branch.shape[2]
    tm = min(MERGE_TM, n_tok)
    tn = MERGE_TN
    nj = d // tn

    def gate_spec(n):
        return pl.BlockSpec((tm, tn), lambda i, j: (i, n * nj + j))

    branch = pl.BlockSpec((tm, width), lambda i, j: (i, 0))
    return pl.pallas_call(
        _merge_kernel,
        grid=(n_tok // tm, nj),
        in_specs=[branch, branch, branch, gate_spec(0), gate_spec(1), gate_spec(2),
                  pl.BlockSpec((N_BRANCHES, width, tn), lambda i, j: (layer, 0, j))],
        out_specs=pl.BlockSpec((tm, tn), lambda i, j: (i, j)),
        out_shape=jax.ShapeDtypeStruct((n_tok, d), _BF16),
        compiler_params=_cparams(2, 48),
        name="branch_merge",
    )(o_a, o_b, o_c, gates, gates, gates, w_branch)


def _layer_norm(y, g, b):
    mu = jnp.mean(y, axis=-1, keepdims=True)
    var = jnp.mean(jnp.square(y - mu), axis=-1, keepdims=True)
    return (y - mu) * lax.rsqrt(var + LN_EPS) * g + b


def _out_router_kernel(gs_ref, wo_ref, x_ref, g_ref, b_ref, wr_ref, br_ref, x1_ref, *, alpha):
    d = x_ref.shape[1]
    x1 = _layer_norm(alpha * x_ref[...] + _dot(gs_ref[...], wo_ref[...]), g_ref[...], b_ref[...])
    x1_ref[:, :d] = x1

    logits = jnp.dot(x1, wr_ref[...], preferred_element_type=_F32,
                     precision=lax.Precision.HIGHEST) + br_ref[...]
    gl = [logits[:, j:j + 1] for j in range(N_GROUPS)]
    gmax = functools.reduce(jnp.maximum, gl)
    gi = jnp.full_like(gmax, N_GROUPS - 1)
    for j in reversed(range(N_GROUPS - 1)):
        gi = jnp.where(gl[j] == gmax, float(j), gi)
    gp = 1.0 / functools.reduce(jnp.add, [jnp.exp(v - gmax) for v in gl])
    el = []
    for k in range(EXPERTS_PER_GROUP):
        v = jnp.zeros_like(gmax)
        for j in range(N_GROUPS):
            c = N_GROUPS + j * EXPERTS_PER_GROUP + k
            v = jnp.where(gi == float(j), logits[:, c:c + 1], v)
        el.append(v)
    e1 = functools.reduce(jnp.maximum, el)
    i1 = jnp.full_like(e1, EXPERTS_PER_GROUP - 1)
    for k in reversed(range(EXPERTS_PER_GROUP - 1)):
        i1 = jnp.where(el[k] == e1, float(k), i1)
    rest = [jnp.where(i1 == float(k), -jnp.inf, el[k]) for k in range(EXPERTS_PER_GROUP)]
    e2 = functools.reduce(jnp.maximum, rest)
    i2 = jnp.full_like(e2, EXPERTS_PER_GROUP - 1)
    for k in reversed(range(EXPERTS_PER_GROUP - 1)):
        i2 = jnp.where((rest[k] == e2) & (i1 != float(k)), float(k), i2)
    ex = jnp.exp(e2 - e1)
    w1 = gp / (1.0 + ex)
    w2 = gp * ex / (1.0 + ex)
    lane = lax.broadcasted_iota(jnp.int32, logits.shape, 1)
    rt = jnp.where(lane == 0, gi, 0.0)
    for k in range(EXPERTS_PER_GROUP):
        wk = jnp.where(i1 == float(k), w1, 0.0) + jnp.where(i2 == float(k), w2, 0.0)
        rt = jnp.where(lane == k + 1, wk, rt)
    x1_ref[:, d:] = rt


def _out_router(gs, w_out, x, ln_g, ln_b, w_r, b_r, layer, alpha):
    n_tok, d = x.shape
    tm = min(OUT_TM, n_tok)
    row = pl.BlockSpec((tm, d), lambda i: (i, 0))
    vec = pl.BlockSpec((None, 1, d), lambda i: (layer, 0, 0))
    return pl.pallas_call(
        functools.partial(_out_router_kernel, alpha=alpha),
        grid=(n_tok // tm,),
        in_specs=[row, pl.BlockSpec((None, d, d), lambda i: (layer, 0, 0)), row, vec, vec,
                  pl.BlockSpec((None, d, LANES), lambda i: (layer, 0, 0)),
                  pl.BlockSpec((None, 1, LANES), lambda i: (layer, 0, 0))],
        out_specs=pl.BlockSpec((tm, d + LANES), lambda i: (i, 0)),
        out_shape=jax.ShapeDtypeStruct((n_tok, d + LANES), _F32),
        compiler_params=_cparams(1, 48),
        name="out_proj_ln_router",
    )(gs, w_out, x, ln_g, ln_b, w_r, b_r)


def _moe_kernel(tile_group_ref, tile_rows_ref, row_token_ref, x_hbm, wg_ref, wu_ref, wd_ref,
                g_ref, b_ref, o_hbm, xg_ref, xb_ref, acc_ref, y_ref, sem, *, alpha):
    i = pl.program_id(0)
    e = pl.program_id(1)
    tm = xg_ref.shape[0]
    d = y_ref.shape[1]
    n_rows = tile_rows_ref[i]
    base = i * tm

    def row_copy_in(r):
        tok = row_token_ref[base + r]
        return pltpu.make_async_copy(x_hbm.at[pl.ds(tok, 1)], xg_ref.at[pl.ds(r, 1)], sem.at[0])

    def row_copy_out(r):
        tok = row_token_ref[base + r]
        return pltpu.make_async_copy(y_ref.at[pl.ds(r, 1)], o_hbm.at[pl.ds(tok, 1)], sem.at[1])

    @pl.when((e == 0) & (n_rows > 0))
    def _():
        def start(r, _):
            row_copy_in(r).start()
            return 0

        def wait(r, _):
            row_copy_in(r).wait()
            return 0

        lax.fori_loop(0, tm, start, 0)
        lax.fori_loop(0, tm, wait, 0)
        xb_ref[...] = xg_ref[:, :d].astype(_BF16)

    @pl.when(n_rows > 0)
    def _():
        xb = xb_ref[...]
        route = xg_ref[:, d:]
        lane = lax.broadcasted_iota(jnp.int32, route.shape, 1)
        cw = jnp.sum(jnp.where(lane == e + 1, route, 0.0), axis=1, keepdims=True)
        hid = jax.nn.silu(_dot(xb, wg_ref[0])) * _dot(xb, wu_ref[0]) * cw
        contrib = _dot(hid.astype(_BF16), wd_ref[0])

        @pl.when(e == 0)
        def _():
            acc_ref[...] = contrib

        @pl.when(e > 0)
        def _():
            acc_ref[...] += contrib

    @pl.when((e == EXPERTS_PER_GROUP - 1) & (n_rows > 0))
    def _():
        y_ref[...] = _layer_norm(alpha * xg_ref[:, :d] + acc_ref[...], g_ref[...], b_ref[...])

        def start(r, _):
            row_copy_out(r).start()
            return 0

        def wait(r, _):
            row_copy_out(r).wait()
            return 0

        lax.fori_loop(0, n_rows, start, 0)
        lax.fori_loop(0, n_rows, wait, 0)


def _moe(x1e, wg, wu, wd, ln_g, ln_b, layer, alpha):
    n_tok = x1e.shape[0]
    d = x1e.shape[1] - LANES
    tm = min(MOE_TM, n_tok)
    n_tiles = n_tok // tm + N_GROUPS
    n_rows = n_tiles * tm

    gi = x1e[:, d].astype(jnp.int32)
    onehot = (gi[:, None] == jnp.arange(N_GROUPS)[None, :]).astype(jnp.int32)
    rank = jnp.sum((jnp.cumsum(onehot, axis=0) - onehot) * onehot, axis=1)
    counts = jnp.sum(onehot, axis=0)
    padded = (counts + tm - 1) // tm * tm
    ends = jnp.cumsum(padded)
    starts = ends - padded
    dest = jnp.sum(onehot * starts[None, :], axis=1) + rank
    row_token = jnp.zeros((n_rows,), jnp.int32).at[dest].set(jnp.arange(n_tok, dtype=jnp.int32))
    tile_start = jnp.arange(n_tiles, dtype=jnp.int32) * tm
    in_group = (tile_start[:, None] >= starts[None, :]) & (tile_start[:, None] < ends[None, :])
    tile_group = jnp.sum(in_group * jnp.arange(N_GROUPS)[None, :], axis=1).astype(jnp.int32)
    tile_rows = jnp.sum(in_group * jnp.clip(starts + counts - tile_start[:, None], 0, tm), axis=1)
    tile_rows = tile_rows.astype(jnp.int32)

    def expert(i, e, tg, tr, rtok):
        return layer * N_EXPERTS + tg[i] * EXPERTS_PER_GROUP + e

    def vec(i, e, tg, tr, rtok):
        return (layer, 0, 0)

    grid_spec = pltpu.PrefetchScalarGridSpec(
        num_scalar_prefetch=3,
        grid=(n_tiles, EXPERTS_PER_GROUP),
        in_specs=[pl.BlockSpec(memory_space=pl.ANY),
                  pl.BlockSpec((1, d, D_EXPERT), lambda i, e, tg, tr, rtok: (expert(i, e, tg, tr, rtok), 0, 0)),
                  pl.BlockSpec((1, d, D_EXPERT), lambda i, e, tg, tr, rtok: (expert(i, e, tg, tr, rtok), 0, 0)),
                  pl.BlockSpec((1, D_EXPERT, d), lambda i, e, tg, tr, rtok: (expert(i, e, tg, tr, rtok), 0, 0)),
                  pl.BlockSpec((None, 1, d), vec),
                  pl.BlockSpec((None, 1, d), vec)],
        out_specs=pl.BlockSpec(memory_space=pl.ANY),
        scratch_shapes=[pltpu.VMEM((tm, d + LANES), _F32),
                        pltpu.VMEM((tm, d), _BF16),
                        pltpu.VMEM((tm, d), _F32),
                        pltpu.VMEM((tm, d), _F32),
                        pltpu.SemaphoreType.DMA((2,))],
    )
    return pl.pallas_call(
        functools.partial(_moe_kernel, alpha=alpha),
        grid_spec=grid_spec,
        out_shape=jax.ShapeDtypeStruct((n_tok, d), _F32),
        compiler_params=_cparams(2, 48),
        name="moe_ln2",
    )(tile_group, tile_rows, row_token, x1e, wg, wu, wd, ln_g, ln_b)


def _in_offsets(d):
    names = ("q_a", "k_c", "v_c", "k_s", "v_s", "k_w", "v_w", "g_nsa", "q_b", "c_kv", "q_idx", "k_idx",
             "w_idx", "uv", "g_merge")
    kv = NSA_KV_HEADS * HEAD_DIM
    sizes = (NSA_HEADS * HEAD_DIM, kv, kv, kv, kv, kv, kv, NSA_HEADS * 3, DSA_HEADS * HEAD_DIM, KV_LATENT,
             IDX_HEADS * IDX_DIM, IDX_DIM, IDX_HEADS, 2 * GMLP_WIDTH, N_BRANCHES * d)
    offs, start = {}, 0
    for name, n in zip(names, sizes):
        offs[name] = start
        start += n
    return offs, start


def _pack_kernel(w_ref, o1_ref, o2_ref, o4_ref, *, offs):
    rows = w_ref.shape[1]

    def put(dst, dst_off, src_off, n, slot=None):
        slot = n if slot is None else slot
        piece = w_ref[0, :, src_off:src_off + n].astype(_BF16)
        if slot > n:
            piece = jnp.concatenate([piece, jnp.zeros((rows, slot - n), _BF16)], axis=1)
        dst[0, :, dst_off:dst_off + slot] = piece

    kv = NSA_KV_HEADS * HEAD_DIM
    put(o1_ref, H1_Q_A, offs["q_a"], NSA_HEADS * HEAD_DIM)
    put(o1_ref, H1_KS, offs["k_s"], 4 * kv)
    put(o1_ref, H1_Q_B, offs["q_b"], DSA_HEADS * HEAD_DIM)
    for h in range(IDX_HEADS):
        put(o1_ref, H1_Q_IDX + h * LANES, offs["q_idx"] + h * IDX_DIM, IDX_DIM, LANES)
    put(o1_ref, H1_K_IDX, offs["k_idx"], IDX_DIM, H1_COLS - H1_K_IDX)

    put(o2_ref, H2_UV, offs["uv"], 2 * GMLP_WIDTH)
    put(o2_ref, H2_KC, offs["k_c"], 2 * kv)
    put(o2_ref, H2_CKV, offs["c_kv"], KV_LATENT)
    per_head = NSA_GROUP * 3
    for h in range(NSA_KV_HEADS):
        put(o2_ref, H2_GN + h * LANES, offs["g_nsa"] + h * per_head, per_head, LANES)
    put(o2_ref, H2_WIDX, offs["w_idx"], IDX_HEADS, LANES)

    put(o4_ref, 0, offs["g_merge"], o4_ref.shape[2])


def _pack_w_in(w_in):
    layers, d, cols = w_in.shape
    offs, total = _in_offsets(d)
    assert total == cols
    tr = 128
    return pl.pallas_call(
        functools.partial(_pack_kernel, offs=offs),
        grid=(layers, d // tr),
        in_specs=[pl.BlockSpec((1, tr, cols), lambda l, r: (l, r, 0))],
        out_specs=[pl.BlockSpec((1, tr, H1_COLS), lambda l, r: (l, r, 0)),
                   pl.BlockSpec((1, tr, H2_COLS), lambda l, r: (l, r, 0)),
                   pl.BlockSpec((1, tr, N_BRANCHES * d), lambda l, r: (l, r, 0))],
        out_shape=[jax.ShapeDtypeStruct((layers, d, H1_COLS), _BF16),
                   jax.ShapeDtypeStruct((layers, d, H2_COLS), _BF16),
                   jax.ShapeDtypeStruct((layers, d, N_BRANCHES * d), _BF16)],
        compiler_params=_cparams(2, 48),
        name="pack_w_in",
    )(w_in)


def kernel(x, w_in, cmp_w1, cmp_w2, cmp_pe, w_uk, w_uv, kv_norm_g, gmlp_ln_g, gmlp_ln_b, gmlp_w_s, gmlp_b_s, w_branch, w_out, ln1_g, ln1_b, router_group_w, router_group_b, router_expert_w, router_expert_b, expert_w_gate, expert_w_up, expert_w_down, ln2_g, ln2_b):
    b, seq, d = x.shape
    depth = w_in.shape[0]
    alpha = float((2 * depth) ** 0.25)
    n_tok = b * seq
    slopes = 2.0 ** (-8.0 * jnp.arange(1, NSA_HEADS + 1, dtype=_F32) / NSA_HEADS)

    w1, w2, w4 = _pack_w_in(w_in)
    pe = cmp_pe.reshape(depth * 2, CMP_BLOCK, HEAD_DIM)
    cw1 = cmp_w1.astype(_BF16).reshape(depth * 2, CMP_BLOCK * HEAD_DIM, HEAD_DIM)
    cw2 = cmp_w2.astype(_BF16).reshape(depth * 2, HEAD_DIM, HEAD_DIM)
    wuk_t = w_uk.transpose(0, 1, 3, 2).astype(_BF16).reshape(depth * DSA_HEADS, HEAD_DIM, KV_LATENT)
    wuv = w_uv.astype(_BF16).reshape(depth * DSA_HEADS, KV_LATENT, HEAD_DIM)
    kvg = kv_norm_g.reshape(depth, 1, KV_LATENT)
    g_ln_g = gmlp_ln_g.reshape(depth, 1, GMLP_WIDTH)
    g_ln_b = gmlp_ln_b.reshape(depth, 1, GMLP_WIDTH)
    g_ws = gmlp_w_s.reshape(depth * GMLP_GROUPS, GMLP_CHUNK, GMLP_CHUNK)
    g_bs_t = gmlp_b_s.transpose(0, 2, 1)
    wb = w_branch.astype(_BF16).reshape(depth * N_BRANCHES, w_branch.shape[2], d)
    wo = w_out.astype(_BF16)
    n_route = N_GROUPS + N_EXPERTS
    w_r = jnp.pad(jnp.concatenate([router_group_w, router_expert_w], axis=2),
                  ((0, 0), (0, 0), (0, LANES - n_route)))
    b_r = jnp.pad(jnp.concatenate([router_group_b, router_expert_b], axis=1),
                  ((0, 0), (0, LANES - n_route))).reshape(depth, 1, LANES)
    wg = expert_w_gate.astype(_BF16).reshape(depth * N_EXPERTS, d, D_EXPERT)
    wu = expert_w_up.astype(_BF16).reshape(depth * N_EXPERTS, d, D_EXPERT)
    wd = expert_w_down.astype(_BF16).reshape(depth * N_EXPERTS, D_EXPERT, d)
    l1g, l1b = ln1_g.reshape(depth, 1, d), ln1_b.reshape(depth, 1, d)
    l2g, l2b = ln2_g.reshape(depth, 1, d), ln2_b.reshape(depth, 1, d)

    xt = x.reshape(n_tok, d)
    for l in range(depth):
        h1 = _matmul(xt, w1, l, _BF16, 768).reshape(b, seq, H1_COLS)
        h2 = _matmul(xt, w2, l, _F32, 640).reshape(b, seq, H2_COLS)
        gates = _matmul(xt, w4, l, _BF16, 768, act="sigmoid")

        kcv = _compress(h2, pe, cw1, cw2, l, seq)
        ocg, selmask = _cmp_select(slopes, h1, kcv, h2, seq)
        o_a = _sel_win(slopes, h1, selmask, ocg, h2, seq)

        cn = _rms_norm(h2, kvg, l, seq)
        o_b = _dsa(slopes, h1, h2, cn, wuk_t, wuv, l, seq)

        o_c = _gmlp(h2, g_ln_g, g_ln_b, g_ws, g_bs_t, l, seq)

        gs = _merge(o_a.reshape(n_tok, -1), o_b.reshape(n_tok, -1), o_c.reshape(n_tok, -1), gates, wb, l)
        x1e = _out_router(gs, wo, xt, l1g, l1b, w_r, b_r, l, alpha)
        xt = _moe(x1e, wg, wu, wd, l2g, l2b, l, alpha)
    return xt.reshape(b, seq, d)
```

```python
import functools

import jax
import jax.numpy as jnp
from jax import lax
from jax.experimental import pallas as pl
from jax.experimental.pallas import tpu as pltpu

HEAD_DIM = 128
NSA_HEADS = 8
NSA_KV_HEADS = 2
NSA_GROUP = NSA_HEADS // NSA_KV_HEADS
CMP_BLOCK = 32
CMP_STRIDE = 16
SEL_BLOCK = 64
N_SEL = 16
WINDOW = 512
FORCE_BONUS = 1e4
DSA_HEADS = 8
KV_LATENT = 256
IDX_HEADS = 4
IDX_DIM = 64
DSA_TOPK_MAX = 256
QUERY_BLOCK = 128
GMLP_GROUPS = 8
GMLP_GROUP_DIM = 128
GMLP_WIDTH = GMLP_GROUPS * GMLP_GROUP_DIM
GMLP_CHUNK = 128
N_BRANCHES = 3
N_GROUPS = 4
EXPERTS_PER_GROUP = 4
N_EXPERTS = N_GROUPS * EXPERTS_PER_GROUP
D_EXPERT = 512
LN_EPS = 1e-5
NEG_BIG = -1e30
ATTN_SCALE = HEAD_DIM ** -0.5

LANES = 128
INT_MIN = -2 ** 31
MIB = 1024 * 1024

H1_Q_A = 0
H1_KS, H1_VS, H1_KW, H1_VW = 1024, 1280, 1536, 1792
H1_Q_B = 2048
H1_Q_IDX = 3072
H1_K_IDX = 3584
H1_COLS = 3840
H2_UV = 0
H2_KC, H2_VC = 2048, 2304
H2_CKV = 2560
H2_GN = 2816
H2_WIDX = 3072
H2_COLS = 3200

MM_TM = 1024
CMP_TQ = 256
SEL_TK = 512
ATT_TQ = 128
WIN_TK = 128
DSA_TK = 512
MERGE_TM = 512
MERGE_TN = 512
OUT_TM = 512
MOE_TM = 512

_F32 = jnp.float32
_BF16 = jnp.bfloat16


def _cparams(n_axes, vmem_mib):
    return pltpu.CompilerParams(dimension_semantics=("arbitrary",) * n_axes,
                                vmem_limit_bytes=vmem_mib * MIB)


def _dot(a, b):
    return jnp.dot(a, b, preferred_element_type=_F32)


def _dot_nt(a, b):
    return lax.dot_general(a, b, (((1,), (1,)), ((), ())), preferred_element_type=_F32)


def _mm_kernel(a_ref, b_ref, o_ref, a_bf_ref, *, act):
    @pl.when(pl.program_id(1) == 0)
    def _():
        a_bf_ref[...] = a_ref[...].astype(_BF16)

    acc = _dot(a_bf_ref[...], b_ref[...])
    if act == "sigmoid":
        acc = jax.nn.sigmoid(acc)
    o_ref[...] = acc.astype(o_ref.dtype)


def _matmul(a, w, layer, out_dtype, tn, act=None):
    m, k = a.shape
    n = w.shape[2]
    tm = min(MM_TM, m)
    return pl.pallas_call(
        functools.partial(_mm_kernel, act=act),
        grid=(m // tm, n // tn),
        in_specs=[pl.BlockSpec((tm, k), lambda i, j: (i, 0)),
                  pl.BlockSpec((None, k, tn), lambda i, j: (layer, 0, j))],
        out_specs=pl.BlockSpec((tm, tn), lambda i, j: (i, j)),
        out_shape=jax.ShapeDtypeStruct((m, n), out_dtype),
        scratch_shapes=[pltpu.VMEM((tm, k), _BF16)],
        compiler_params=_cparams(2, 48),
        name="proj_matmul",
    )(a, w)


def _compress_kernel(x_ref, pe_ref, w1_ref, w2_ref, o_ref):
    nc = o_ref.shape[3]
    top = jnp.zeros((nc, HEAD_DIM), _F32)
    bot = jnp.zeros((nc, HEAD_DIM), _F32)
    for ll in range(CMP_STRIDE):
        y = x_ref[0, pl.ds(ll, nc, stride=CMP_STRIDE), :]
        lo, hi = ll, CMP_STRIDE + ll
        top = top + _dot((y + pe_ref[0, lo:lo + 1, :]).astype(_BF16),
                         w1_ref[0, lo * HEAD_DIM:(lo + 1) * HEAD_DIM, :])
        bot = bot + _dot((y + pe_ref[0, hi:hi + 1, :]).astype(_BF16),
                         w1_ref[0, hi * HEAD_DIM:(hi + 1) * HEAD_DIM, :])
    pre = top + pltpu.roll(bot, nc - 1, 0)
    o_ref[0, 0, 0] = _dot(jax.nn.gelu(pre).astype(_BF16), w2_ref[0]).astype(o_ref.dtype)


def _compress(h2, pe, w1, w2, layer, seq):
    b = h2.shape[0]
    nc = seq // CMP_STRIDE
    col = H2_KC // HEAD_DIM
    return pl.pallas_call(
        _compress_kernel,
        grid=(b, 2, NSA_KV_HEADS),
        in_specs=[pl.BlockSpec((1, seq, HEAD_DIM), lambda i, j, h: (i, 0, col + j * NSA_KV_HEADS + h)),
                  pl.BlockSpec((1, CMP_BLOCK, HEAD_DIM), lambda i, j, h: (2 * layer + j, 0, 0)),
                  pl.BlockSpec((1, CMP_BLOCK * HEAD_DIM, HEAD_DIM), lambda i, j, h: (2 * layer + j, 0, 0)),
                  pl.BlockSpec((1, HEAD_DIM, HEAD_DIM), lambda i, j, h: (2 * layer + j, 0, 0))],
        out_specs=pl.BlockSpec((1, 1, 1, nc, HEAD_DIM), lambda i, j, h: (i, j, h, 0, 0)),
        out_shape=jax.ShapeDtypeStruct((b, 2, NSA_KV_HEADS, nc, HEAD_DIM), _BF16),
        compiler_params=_cparams(3, 32),
        name="nsa_compress",
    )(h2, pe, w1, w2)


def _cmp_select_kernel(slopes_ref, q_ref, kc_ref, vc_ref, gn_ref, ocg_ref, mask_ref, *, seq):
    h = pl.program_id(1)
    qt = pl.program_id(2)
    tq = q_ref.shape[1]
    nc = kc_ref.shape[3]
    n_cmp = (seq - CMP_BLOCK) // CMP_STRIDE + 1
    n_slc = seq // SEL_BLOCK

    t_col = qt * tq + lax.broadcasted_iota(jnp.int32, (tq, 1), 0)
    i_row = lax.broadcasted_iota(jnp.int32, (1, nc), 1)
    dist_i = t_col - (i_row * CMP_STRIDE + (CMP_BLOCK - 1))
    valid = (dist_i >= 0) & (i_row < n_cmp)
    dist = dist_i.astype(_F32)

    kc = kc_ref[0, 0, 0]
    vc = vc_ref[0, 0, 0]
    gn = jax.nn.sigmoid(gn_ref[0])
    psum = jnp.zeros((tq, nc), _F32)
    for g in range(NSA_GROUP):
        qg = q_ref[0, :, g * HEAD_DIM:(g + 1) * HEAD_DIM]
        slope = slopes_ref[h * NSA_GROUP + g]
        s = _dot_nt(qg, kc) * ATTN_SCALE - slope * dist
        s = jnp.where(valid, s, NEG_BIG)
        e = jnp.exp(s - jnp.max(s, axis=1, keepdims=True))
        p = e / jnp.sum(e, axis=1, keepdims=True)
        p = jnp.where(valid, p, 0.0)
        o = _dot(p.astype(_BF16), vc)
        ocg_ref[0, :, g * HEAD_DIM:(g + 1) * HEAD_DIM] = gn[:, 3 * g:3 * g + 1] * o
        psum = psum + p

    ci = lax.broadcasted_iota(jnp.int32, (nc, LANES), 0) * CMP_STRIDE
    sj = lax.broadcasted_iota(jnp.int32, (nc, LANES), 1) * SEL_BLOCK
    overlap = ((ci < sj + SEL_BLOCK) & (ci + CMP_BLOCK > sj)).astype(_F32)
    imp = jnp.dot(psum, overlap, preferred_element_type=_F32, precision=lax.Precision.HIGHEST)

    j_row = lax.broadcasted_iota(jnp.int32, (1, LANES), 1)
    cur = t_col >> (SEL_BLOCK.bit_length() - 1)
    forced = (j_row == 0) | (j_row == cur) | (j_row == cur - 1)
    imp = jnp.where(j_row <= cur, imp + jnp.where(forced, FORCE_BONUS, 0.0), NEG_BIG)
    imp = jnp.where(j_row < n_slc, imp, -3e38)
    rank = jnp.zeros((tq, LANES), _F32)
    for jp in range(n_slc):
        col = imp[:, jp:jp + 1]
        before = (col > imp) | ((col == imp) & (j_row > jp))
        rank = rank + before.astype(_F32)
    sel = (rank < min(N_SEL, n_slc)).astype(_BF16)

    tk = mask_ref.shape[4]
    bj = lax.broadcasted_iota(jnp.int32, (LANES, tk), 0)
    ks = lax.broadcasted_iota(jnp.int32, (LANES, tk), 1)
    for kt in range(mask_ref.shape[2]):
        expand = (bj == ((ks + kt * tk) >> (SEL_BLOCK.bit_length() - 1))).astype(_BF16)
        mask_ref[0, 0, kt] = _dot(sel, expand).astype(mask_ref.dtype)


def _cmp_select(slopes, h1, kcv, h2, seq):
    b = h1.shape[0]
    nc = kcv.shape[3]
    tq = min(CMP_TQ, seq)
    tk = min(SEL_TK, seq)
    nkt = seq // tk
    qw = NSA_GROUP * HEAD_DIM
    return pl.pallas_call(
        functools.partial(_cmp_select_kernel, seq=seq),
        grid=(b, NSA_KV_HEADS, seq // tq),
        in_specs=[pl.BlockSpec(memory_space=pltpu.SMEM),
                  pl.BlockSpec((1, tq, qw), lambda i, h, t: (i, t, h)),
                  pl.BlockSpec((1, 1, 1, nc, HEAD_DIM), lambda i, h, t: (i, 0, h, 0, 0)),
                  pl.BlockSpec((1, 1, 1, nc, HEAD_DIM), lambda i, h, t: (i, 1, h, 0, 0)),
                  pl.BlockSpec((1, tq, LANES), lambda i, h, t: (i, t, H2_GN // LANES + h))],
        out_specs=[pl.BlockSpec((1, tq, qw), lambda i, h, t: (i, t, h)),
                   pl.BlockSpec((1, 1, nkt, tq, tk), lambda i, h, t: (i, h, 0, t, 0))],
        out_shape=[jax.ShapeDtypeStruct((b, seq, NSA_HEADS * HEAD_DIM), _F32),
                   jax.ShapeDtypeStruct((b, NSA_KV_HEADS, nkt, seq, tk), _BF16)],
        compiler_params=_cparams(3, 32),
        name="nsa_cmp_select",
    )(slopes, h1, kcv, kcv, h2)


def _flash(q4, k_ref, v_ref, kt_lo, kt_hi, tk, t4, slope4, mask_fn):
    rows = q4.shape[0]

    def body(kt, carry):
        m, l, acc = carry
        start = pl.multiple_of(kt * tk, tk)
        k = k_ref[0, pl.ds(start, tk), :]
        v = v_ref[0, pl.ds(start, tk), :]
        s_pos = start + lax.broadcasted_iota(jnp.int32, (1, tk), 1)
        dist_i = t4 - s_pos
        valid = mask_fn(kt, dist_i)
        s = _dot_nt(q4, k) * ATTN_SCALE - slope4 * dist_i.astype(_F32)
        s = jnp.where(valid, s, NEG_BIG)
        m_new = jnp.maximum(m, jnp.max(s, axis=1, keepdims=True))
        p = jnp.where(valid, jnp.exp(s - m_new), 0.0)
        alpha = jnp.exp(m - m_new)
        l = alpha * l + jnp.sum(p, axis=1, keepdims=True)
        acc = alpha * acc + _dot(p.astype(_BF16), v)
        return m_new, l, acc

    init = (jnp.full((rows, 1), NEG_BIG, _F32), jnp.zeros((rows, 1), _F32),
            jnp.zeros((rows, HEAD_DIM), _F32))
    _, l, acc = lax.fori_loop(kt_lo, kt_hi + 1, body, init)
    return acc * jnp.where(l > 0, 1.0 / l, 0.0)


def _sel_win_kernel(slopes_ref, q_ref, ks_ref, vs_ref, kw_ref, vw_ref, mask_ref, ocg_ref, gn_ref,
                    o_ref):
    h = pl.program_id(1)
    qt = pl.program_id(2)
    tq = q_ref.shape[1]
    rows = NSA_GROUP * tq
    q0 = qt * tq
    q4 = jnp.concatenate([q_ref[0, :, g * HEAD_DIM:(g + 1) * HEAD_DIM] for g in range(NSA_GROUP)],
                         axis=0)
    row = lax.broadcasted_iota(jnp.int32, (rows, 1), 0)
    t4 = q0 + (row & (tq - 1))
    g4 = row >> (tq.bit_length() - 1)
    slope4 = jnp.zeros((rows, 1), _F32)
    for g in range(NSA_GROUP):
        slope4 = jnp.where(g4 == g, slopes_ref[h * NSA_GROUP + g], slope4)

    sel_tk = mask_ref.shape[4]

    def sel_mask(kt, dist_i):
        mk = mask_ref[0, 0, kt].astype(_F32)
        mk4 = jnp.concatenate([mk] * NSA_GROUP, axis=0)
        return (mk4 > 0.5) & (dist_i >= 0)

    def win_mask(kt, dist_i):
        return (dist_i >= 0) & (dist_i < WINDOW)

    o_slc = _flash(q4, ks_ref, vs_ref, 0, (q0 + tq - 1) // sel_tk, sel_tk, t4, slope4, sel_mask)
    span = WINDOW + tq
    w0 = pl.multiple_of(jnp.maximum(q0 - WINDOW, 0), tq)
    kwin = kw_ref[0, pl.ds(w0, span), :]
    vwin = vw_ref[0, pl.ds(w0, span), :]
    dist_w = t4 - (w0 + lax.broadcasted_iota(jnp.int32, (1, span), 1))
    valid_w = win_mask(0, dist_w)
    s_w = _dot_nt(q4, kwin) * ATTN_SCALE - slope4 * dist_w.astype(_F32)
    s_w = jnp.where(valid_w, s_w, NEG_BIG)
    p_w = jnp.where(valid_w, jnp.exp(s_w - jnp.max(s_w, axis=1, keepdims=True)), 0.0)
    l_w = jnp.sum(p_w, axis=1, keepdims=True)
    o_win = _dot(p_w.astype(_BF16), vwin) * jnp.where(l_w > 0, 1.0 / l_w, 0.0)

    gn = jax.nn.sigmoid(gn_ref[0])
    for g in range(NSA_GROUP):
        o = (ocg_ref[0, :, g * HEAD_DIM:(g + 1) * HEAD_DIM]
             + gn[:, 3 * g + 1:3 * g + 2] * o_slc[g * tq:(g + 1) * tq]
             + gn[:, 3 * g + 2:3 * g + 3] * o_win[g * tq:(g + 1) * tq])
        o_ref[0, :, g * HEAD_DIM:(g + 1) * HEAD_DIM] = o.astype(o_ref.dtype)


def _sel_win(slopes, h1, selmask, ocg, h2, seq):
    b = h1.shape[0]
    tq = min(ATT_TQ, seq)
    nkt, tk = selmask.shape[2], selmask.shape[4]
    qw = NSA_GROUP * HEAD_DIM

    def kv_spec(col):
        return pl.BlockSpec((1, seq, HEAD_DIM), lambda i, h, t: (i, 0, col // HEAD_DIM + h))

    return pl.pallas_call(
        _sel_win_kernel,
        grid=(b, NSA_KV_HEADS, seq // tq),
        in_specs=[pl.BlockSpec(memory_space=pltpu.SMEM),
                  pl.BlockSpec((1, tq, qw), lambda i, h, t: (i, t, h)),
                  kv_spec(H1_KS), kv_spec(H1_VS), kv_spec(H1_KW), kv_spec(H1_VW),
                  pl.BlockSpec((1, 1, nkt, tq, tk), lambda i, h, t: (i, h, 0, t, 0)),
                  pl.BlockSpec((1, tq, qw), lambda i, h, t: (i, t, h)),
                  pl.BlockSpec((1, tq, LANES), lambda i, h, t: (i, t, H2_GN // LANES + h))],
        out_specs=pl.BlockSpec((1, tq, qw), lambda i, h, t: (i, t, h)),
        out_shape=jax.ShapeDtypeStruct((b, seq, NSA_HEADS * HEAD_DIM), _BF16),
        compiler_params=_cparams(3, 32),
        name="nsa_sel_win",
    )(slopes, h1, h1, h1, h1, h1, selmask, ocg, h2)


def _rms_kernel(c_ref, g_ref, o_ref):
    c = c_ref[0]
    o_ref[0] = (c * lax.rsqrt(jnp.mean(c * c, axis=-1, keepdims=True) + LN_EPS) * g_ref[...]
                ).astype(o_ref.dtype)


def _rms_norm(h2, g, layer, seq):
    b = h2.shape[0]
    ts = min(512, seq)
    return pl.pallas_call(
        _rms_kernel,
        grid=(b, seq // ts),
        in_specs=[pl.BlockSpec((1, ts, KV_LATENT), lambda i, t: (i, t, H2_CKV // KV_LATENT)),
                  pl.BlockSpec((None, 1, KV_LATENT), lambda i, t: (layer, 0, 0))],
        out_specs=pl.BlockSpec((1, ts, KV_LATENT), lambda i, t: (i, t, 0)),
        out_shape=jax.ShapeDtypeStruct((b, seq, KV_LATENT), _BF16),
        compiler_params=_cparams(2, 32),
        name="dsa_rms_norm",
    )(h2, g)


def _dsa_kernel(slopes_ref, qb_ref, qi_ref, ki_ref, wi_ref, cn_ref, wuk_ref, wuv_ref, o_ref,
                key_ref, mask_ref, qlat_ref, m_ref, l_ref, acc_ref, *, topk):
    qt = pl.program_id(1)
    tq = qb_ref.shape[1]
    seq = ki_ref.shape[1]
    tk = mask_ref.shape[2]
    q0 = qt * tq
    rows = DSA_HEADS * tq

    t_col = q0 + lax.broadcasted_iota(jnp.int32, (tq, 1), 0)
    s_row = lax.broadcasted_iota(jnp.int32, (1, seq), 1)
    causal = s_row <= t_col
    ki = ki_ref[0]
    wi = wi_ref[0]
    score = jnp.zeros((tq, seq), _F32)
    for hh in range(IDX_HEADS):
        rel = jnp.maximum(_dot_nt(qi_ref[0, :, hh * LANES:(hh + 1) * LANES], ki), 0.0)
        score = score + wi[:, hh:hh + 1] * rel
    score = jnp.where(score == 0.0, 0.0, score)
    bits = lax.bitcast_convert_type(score, jnp.int32)
    key = jnp.where(bits < 0, bits ^ jnp.int32(0x7FFFFFFF), bits)
    key_ref[...] = jnp.where(causal, key, jnp.int32(INT_MIN))

    def radix(i, thr):
        cand = thr ^ jnp.left_shift(jnp.int32(1), 31 - i)
        cnt = jnp.sum((key_ref[...] >= cand).astype(_F32), axis=1, keepdims=True)
        return jnp.where(cnt >= topk, cand, thr)

    thr = lax.fori_loop(0, 32, radix, jnp.full((tq, 1), INT_MIN, jnp.int32))

    key = key_ref[...]
    above = key > thr
    tie = key == thr
    need = topk - jnp.sum(above.astype(_F32), axis=1, keepdims=True)
    tri = (lax.broadcasted_iota(jnp.int32, (LANES, LANES), 0)
           <= lax.broadcasted_iota(jnp.int32, (LANES, LANES), 1)).astype(_BF16)
    run = jnp.zeros((tq, 1), _F32)
    per = tk // LANES
    for c in range(seq // LANES):
        sl = slice(c * LANES, (c + 1) * LANES)
        tie_c = tie[:, sl]
        prefix = _dot(tie_c.astype(_F32).astype(_BF16), tri) + run
        run = prefix[:, LANES - 1:LANES]
        chosen = (above[:, sl] | (tie_c & (prefix <= need))) & causal[:, sl]
        mask_ref[c // per, :, (c % per) * LANES:(c % per + 1) * LANES] = chosen.astype(_F32)

    for hh in range(DSA_HEADS):
        ql = _dot(qb_ref[0, :, hh * HEAD_DIM:(hh + 1) * HEAD_DIM], wuk_ref[hh]) * ATTN_SCALE
        qlat_ref[hh * tq:(hh + 1) * tq, :] = ql.astype(_BF16)

    row = lax.broadcasted_iota(jnp.int32, (rows, 1), 0)
    t8 = q0 + (row & (tq - 1))
    h8 = row >> (tq.bit_length() - 1)
    slope8 = jnp.zeros((rows, 1), _F32)
    for hh in range(DSA_HEADS):
        slope8 = jnp.where(h8 == hh, slopes_ref[hh], slope8)

    m_ref[...] = jnp.full((rows, 1), NEG_BIG, _F32)
    l_ref[...] = jnp.zeros((rows, 1), _F32)
    acc_ref[...] = jnp.zeros((rows, KV_LATENT), _F32)

    def attend(kc, _):
        start = pl.multiple_of(kc * tk, tk)
        c = cn_ref[0, pl.ds(start, tk), :]
        dist = (t8 - (start + lax.broadcasted_iota(jnp.int32, (1, tk), 1))).astype(_F32)
        mk = mask_ref[kc]
        valid = jnp.concatenate([mk] * DSA_HEADS, axis=0) > 0.5
        s = _dot_nt(qlat_ref[...], c) - slope8 * dist
        s = jnp.where(valid, s, NEG_BIG)
        m_old = m_ref[...]
        m_new = jnp.maximum(m_old, jnp.max(s, axis=1, keepdims=True))
        p = jnp.where(valid, jnp.exp(s - m_new), 0.0)
        alpha = jnp.exp(m_old - m_new)
        l_ref[...] = alpha * l_ref[...] + jnp.sum(p, axis=1, keepdims=True)
        acc_ref[...] = alpha * acc_ref[...] + _dot(p.astype(_BF16), c)
        m_ref[...] = m_new
        return 0

    lax.fori_loop(0, (q0 + tq - 1) // tk + 1, attend, 0)

    l = l_ref[...]
    o_lat = (acc_ref[...] * jnp.where(l > 0, 1.0 / l, 0.0)).astype(_BF16)
    for hh in range(DSA_HEADS):
        o = _dot(o_lat[hh * tq:(hh + 1) * tq], wuv_ref[hh])
        o_ref[0, :, hh * HEAD_DIM:(hh + 1) * HEAD_DIM] = o.astype(o_ref.dtype)


def _dsa(slopes, h1, h2, cn, wuk_t, wuv, layer, seq):
    b = h1.shape[0]
    tq = QUERY_BLOCK
    tk = min(DSA_TK, seq)
    topk = min(DSA_TOPK_MAX, seq // 4)
    width = DSA_HEADS * HEAD_DIM
    rows = DSA_HEADS * tq
    return pl.pallas_call(
        functools.partial(_dsa_kernel, topk=topk),
        grid=(b, seq // tq),
        in_specs=[pl.BlockSpec(memory_space=pltpu.SMEM),
                  pl.BlockSpec((1, tq, width), lambda i, t: (i, t, H1_Q_B // width)),
                  pl.BlockSpec((1, tq, IDX_HEADS * LANES),
                               lambda i, t: (i, t, H1_Q_IDX // (IDX_HEADS * LANES))),
                  pl.BlockSpec((1, seq, LANES), lambda i, t: (i, 0, H1_K_IDX // LANES)),
                  pl.BlockSpec((1, tq, LANES), lambda i, t: (i, t, H2_WIDX // LANES)),
                  pl.BlockSpec((1, seq, KV_LATENT), lambda i, t: (i, 0, 0)),
                  pl.BlockSpec((DSA_HEADS, HEAD_DIM, KV_LATENT), lambda i, t: (layer, 0, 0)),
                  pl.BlockSpec((DSA_HEADS, KV_LATENT, HEAD_DIM), lambda i, t: (layer, 0, 0))],
        out_specs=pl.BlockSpec((1, tq, width), lambda i, t: (i, t, 0)),
        out_shape=jax.ShapeDtypeStruct((b, seq, width), _BF16),
        scratch_shapes=[pltpu.VMEM((tq, seq), jnp.int32),
                        pltpu.VMEM((seq // tk, tq, tk), _F32),
                        pltpu.VMEM((rows, KV_LATENT), _BF16),
                        pltpu.VMEM((rows, 1), _F32),
                        pltpu.VMEM((rows, 1), _F32),
                        pltpu.VMEM((rows, KV_LATENT), _F32)],
        compiler_params=_cparams(2, 48),
        name="dsa_attention",
    )(slopes, h1, h1, h1, h2, cn, wuk_t, wuv)


def _gmlp_kernel(uv_ref, g_ref, b_ref, ws_ref, bs_ref, o_ref):
    z = jax.nn.gelu(uv_ref[0])
    u = z[:, :GMLP_WIDTH]
    v = z[:, GMLP_WIDTH:]
    mu = jnp.mean(v, axis=-1, keepdims=True)
    var = jnp.mean(jnp.square(v - mu), axis=-1, keepdims=True)
    vn = ((v - mu) * lax.rsqrt(var + LN_EPS) * g_ref[...] + b_ref[...]).astype(_BF16)
    t = ws_ref.shape[1]
    causal = (lax.broadcasted_iota(jnp.int32, (t, t), 0) >= lax.broadcasted_iota(jnp.int32, (t, t), 1))
    for g in range(GMLP_GROUPS):
        sl = slice(g * GMLP_GROUP_DIM, (g + 1) * GMLP_GROUP_DIM)
        w = jnp.where(causal, ws_ref[g], 0.0).astype(_BF16)
        mixed = _dot(w, vn[:, sl]) + bs_ref[:, g:g + 1]
        o_ref[0, :, sl] = (u[:, sl] * mixed).astype(o_ref.dtype)


def _gmlp(h2, ln_g, ln_b, w_s, b_s_t, layer, seq):
    b = h2.shape[0]
    t = GMLP_CHUNK
    return pl.pallas_call(
        _gmlp_kernel,
        grid=(b, seq // t),
        in_specs=[pl.BlockSpec((1, t, 2 * GMLP_WIDTH), lambda i, c: (i, c, H2_UV // (2 * GMLP_WIDTH))),
                  pl.BlockSpec((None, 1, GMLP_WIDTH), lambda i, c: (layer, 0, 0)),
                  pl.BlockSpec((None, 1, GMLP_WIDTH), lambda i, c: (layer, 0, 0)),
                  pl.BlockSpec((GMLP_GROUPS, t, t), lambda i, c: (layer, 0, 0)),
                  pl.BlockSpec((None, t, GMLP_GROUPS), lambda i, c: (layer, 0, 0))],
        out_specs=pl.BlockSpec((1, t, GMLP_WIDTH), lambda i, c: (i, c, 0)),
        out_shape=jax.ShapeDtypeStruct((b, seq, GMLP_WIDTH), _BF16),
        compiler_params=_cparams(2, 32),
        name="gmlp",
    )(h2, ln_g, ln_b, w_s, b_s_t)


def _merge_kernel(oa_ref, ob_ref, oc_ref, ga_ref, gb_ref, gc_ref, wb_ref, o_ref):
    y = ga_ref[...].astype(_F32) * _dot(oa_ref[...], wb_ref[0])
    y = y + gb_ref[...].astype(_F32) * _dot(ob_ref[...], wb_ref[1])
    y = y + gc_ref[...].astype(_F32) * _dot(oc_ref[...], wb_ref[2])
    o_ref[...] = y.astype(o_ref.dtype)


def _merge(o_a, o_b, o_c, gates, w_branch, layer):
    n_tok, width = o_a.shape
    d = w_branch.shape[2]
    tm = min(MERGE_TM, n_tok)
    tn = MERGE_TN
    nj = d // tn

    def gate_spec(n):
        return pl.BlockSpec((tm, tn), lambda i, j: (i, n * nj + j))

    branch = pl.BlockSpec((tm, width), lambda i, j: (i, 0))
    return pl.pallas_call(
        _merge_kernel,
        grid=(n_tok // tm, nj),
        in_specs=[branch, branch, branch, gate_spec(0), gate_spec(1), gate_spec(2),
                  pl.BlockSpec((N_BRANCHES, width, tn), lambda i, j: (layer, 0, j))],
        out_specs=pl.BlockSpec((tm, tn), lambda i, j: (i, j)),
        out_shape=jax.ShapeDtypeStruct((n_tok, d), _BF16),
        compiler_params=_cparams(2, 48),
        name="branch_merge",
    )(o_a, o_b, o_c, gates, gates, gates, w_branch)


def _layer_norm(y, g, b):
    mu = jnp.mean(y, axis=-1, keepdims=True)
    var = jnp.mean(jnp.square(y - mu), axis=-1, keepdims=True)
    return (y - mu) * lax.rsqrt(var + LN_EPS) * g + b


def _out_router_kernel(gs_ref, wo_ref, x_ref, g_ref, b_ref, wr_ref, br_ref, x1_ref, *, alpha):
    d = x_ref.shape[1]
    x1 = _layer_norm(alpha * x_ref[...] + _dot(gs_ref[...], wo_ref[...]), g_ref[...], b_ref[...])
    x1_ref[:, :d] = x1

    logits = jnp.dot(x1, wr_ref[...], preferred_element_type=_F32,
                     precision=lax.Precision.HIGHEST) + br_ref[...]
    gl = [logits[:, j:j + 1] for j in range(N_GROUPS)]
    gmax = functools.reduce(jnp.maximum, gl)
    gi = jnp.full_like(gmax, N_GROUPS - 1)
    for j in reversed(range(N_GROUPS - 1)):
        gi = jnp.where(gl[j] == gmax, float(j), gi)
    gp = 1.0 / functools.reduce(jnp.add, [jnp.exp(v - gmax) for v in gl])
    el = []
    for k in range(EXPERTS_PER_GROUP):
        v = jnp.zeros_like(gmax)
        for j in range(N_GROUPS):
            c = N_GROUPS + j * EXPERTS_PER_GROUP + k
            v = jnp.where(gi == float(j), logits[:, c:c + 1], v)
        el.append(v)
    e1 = functools.reduce(jnp.maximum, el)
    i1 = jnp.full_like(e1, EXPERTS_PER_GROUP - 1)
    for k in reversed(range(EXPERTS_PER_GROUP - 1)):
        i1 = jnp.where(el[k] == e1, float(k), i1)
    rest = [jnp.where(i1 == float(k), -jnp.inf, el[k]) for k in range(EXPERTS_PER_GROUP)]
    e2 = functools.reduce(jnp.maximum, rest)
    i2 = jnp.full_like(e2, EXPERTS_PER_GROUP - 1)
    for k in reversed(range(EXPERTS_PER_GROUP - 1)):
        i2 = jnp.where((rest[k] == e2) & (i1 != float(k)), float(k), i2)
    ex = jnp.exp(e2 - e1)
    w1 = gp / (1.0 + ex)
    w2 = gp * ex / (1.0 + ex)
    lane = lax.broadcasted_iota(jnp.int32, logits.shape, 1)
    rt = jnp.where(lane == 0, gi, 0.0)
    for k in range(EXPERTS_PER_GROUP):
        wk = jnp.where(i1 == float(k), w1, 0.0) + jnp.where(i2 == float(k), w2, 0.0)
        rt = jnp.where(lane == k + 1, wk, rt)
    x1_ref[:, d:] = rt


def _out_router(gs, w_out, x, ln_g, ln_b, w_r, b_r, layer, alpha):
    n_tok, d = x.shape
    tm = min(OUT_TM, n_tok)
    row = pl.BlockSpec((tm, d), lambda i: (i, 0))
    vec = pl.BlockSpec((None, 1, d), lambda i: (layer, 0, 0))
    return pl.pallas_call(
        functools.partial(_out_router_kernel, alpha=alpha),
        grid=(n_tok // tm,),
        in_specs=[row, pl.BlockSpec((None, d, d), lambda i: (layer, 0, 0)), row, vec, vec,
                  pl.BlockSpec((None, d, LANES), lambda i: (layer, 0, 0)),
                  pl.BlockSpec((None, 1, LANES), lambda i: (layer, 0, 0))],
        out_specs=pl.BlockSpec((tm, d + LANES), lambda i: (i, 0)),
        out_shape=jax.ShapeDtypeStruct((n_tok, d + LANES), _F32),
        compiler_params=_cparams(1, 48),
        name="out_proj_ln_router",
    )(gs, w_out, x, ln_g, ln_b, w_r, b_r)


def _moe_kernel(tile_group_ref, tile_rows_ref, row_token_ref, x_hbm, wg_ref, wu_ref, wd_ref,
                g_ref, b_ref, o_hbm, xg_ref, xb_ref, acc_ref, y_ref, sem, *, alpha):
    i = pl.program_id(0)
    e = pl.program_id(1)
    tm = xg_ref.shape[0]
    d = y_ref.shape[1]
    n_rows = tile_rows_ref[i]
    base = i * tm

    def row_copy_in(r):
        tok = row_token_ref[base + r]
        return pltpu.make_async_copy(x_hbm.at[pl.ds(tok, 1)], xg_ref.at[pl.ds(r, 1)], sem.at[0])

    def row_copy_out(r):
        tok = row_token_ref[base + r]
        return pltpu.make_async_copy(y_ref.at[pl.ds(r, 1)], o_hbm.at[pl.ds(tok, 1)], sem.at[1])

    @pl.when((e == 0) & (n_rows > 0))
    def _():
        def start(r, _):
            row_copy_in(r).start()
            return 0

        def wait(r, _):
            row_copy_in(r).wait()
            return 0

        lax.fori_loop(0, tm, start, 0)
        lax.fori_loop(0, tm, wait, 0)
        xb_ref[...] = xg_ref[:, :d].astype(_BF16)

    @pl.when(n_rows > 0)
    def _():
        xb = xb_ref[...]
        route = xg_ref[:, d:]
        lane = lax.broadcasted_iota(jnp.int32, route.shape, 1)
        cw = jnp.sum(jnp.where(lane == e + 1, route, 0.0), axis=1, keepdims=True)
        hid = jax.nn.silu(_dot(xb, wg_ref[0])) * _dot(xb, wu_ref[0]) * cw
        contrib = _dot(hid.astype(_BF16), wd_ref[0])

        @pl.when(e == 0)
        def _():
            acc_ref[...] = contrib

        @pl.when(e > 0)
        def _():
            acc_ref[...] += contrib

    @pl.when((e == EXPERTS_PER_GROUP - 1) & (n_rows > 0))
    def _():
        y_ref[...] = _layer_norm(alpha * xg_ref[:, :d] + acc_ref[...], g_ref[...], b_ref[...])

        def start(r, _):
            row_copy_out(r).start()
            return 0

        def wait(r, _):
            row_copy_out(r).wait()
            return 0

        lax.fori_loop(0, n_rows, start, 0)
        lax.fori_loop(0, n_rows, wait, 0)


def _moe(x1e, wg, wu, wd, ln_g, ln_b, layer, alpha):
    n_tok = x1e.shape[0]
    d = x1e.shape[1] - LANES
    tm = min(MOE_TM, n_tok)
    n_tiles = n_tok // tm + N_GROUPS
    n_rows = n_tiles * tm

    gi = x1e[:, d].astype(jnp.int32)
    onehot = (gi[:, None] == jnp.arange(N_GROUPS)[None, :]).astype(jnp.int32)
    rank = jnp.sum((jnp.cumsum(onehot, axis=0) - onehot) * onehot, axis=1)
    counts = jnp.sum(onehot, axis=0)
    padded = (counts + tm - 1) // tm * tm
    ends = jnp.cumsum(padded)
    starts = ends - padded
    dest = jnp.sum(onehot * starts[None, :], axis=1) + rank
    row_token = jnp.zeros((n_rows,), jnp.int32).at[dest].set(jnp.arange(n_tok, dtype=jnp.int32))
    tile_start = jnp.arange(n_tiles, dtype=jnp.int32) * tm
    in_group = (tile_start[:, None] >= starts[None, :]) & (tile_start[:, None] < ends[None, :])
    tile_group = jnp.sum(in_group * jnp.arange(N_GROUPS)[None, :], axis=1).astype(jnp.int32)
    tile_rows = jnp.sum(in_group * jnp.clip(starts + counts - tile_start[:, None], 0, tm), axis=1)
    tile_rows = tile_rows.astype(jnp.int32)

    def expert(i, e, tg, tr, rtok):
        return layer * N_EXPERTS + tg[i] * EXPERTS_PER_GROUP + e

    def vec(i, e, tg, tr, rtok):
        return (layer, 0, 0)

    grid_spec = pltpu.PrefetchScalarGridSpec(
        num_scalar_prefetch=3,
        grid=(n_tiles, EXPERTS_PER_GROUP),
        in_specs=[pl.BlockSpec(memory_space=pl.ANY),
                  pl.BlockSpec((1, d, D_EXPERT), lambda i, e, tg, tr, rtok: (expert(i, e, tg, tr, rtok), 0, 0)),
                  pl.BlockSpec((1, d, D_EXPERT), lambda i, e, tg, tr, rtok: (expert(i, e, tg, tr, rtok), 0, 0)),
                  pl.BlockSpec((1, D_EXPERT, d), lambda i, e, tg, tr, rtok: (expert(i, e, tg, tr, rtok), 0, 0)),
                  pl.BlockSpec((None, 1, d), vec),
                  pl.BlockSpec((None, 1, d), vec)],
        out_specs=pl.BlockSpec(memory_space=pl.ANY),
        scratch_shapes=[pltpu.VMEM((tm, d + LANES), _F32),
                        pltpu.VMEM((tm, d), _BF16),
                        pltpu.VMEM((tm, d), _F32),
                        pltpu.VMEM((tm, d), _F32),
                        pltpu.SemaphoreType.DMA((2,))],
    )
    return pl.pallas_call(
        functools.partial(_moe_kernel, alpha=alpha),
        grid_spec=grid_spec,
        out_shape=jax.ShapeDtypeStruct((n_tok, d), _F32),
        compiler_params=_cparams(2, 48),
        name="moe_ln2",
    )(tile_group, tile_rows, row_token, x1e, wg, wu, wd, ln_g, ln_b)


def _in_offsets(d):
    names = ("q_a", "k_c", "v_c", "k_s", "v_s", "k_w", "v_w", "g_nsa", "q_b", "c_kv", "q_idx", "k_idx",
             "w_idx", "uv", "g_merge")
    kv = NSA_KV_HEADS * HEAD_DIM
    sizes = (NSA_HEADS * HEAD_DIM, kv, kv, kv, kv, kv, kv, NSA_HEADS * 3, DSA_HEADS * HEAD_DIM, KV_LATENT,
             IDX_HEADS * IDX_DIM, IDX_DIM, IDX_HEADS, 2 * GMLP_WIDTH, N_BRANCHES * d)
    offs, start = {}, 0
    for name, n in zip(names, sizes):
        offs[name] = start
        start += n
    return offs, start


def _pack_kernel(w_ref, o1_ref, o2_ref, o4_ref, *, offs):
    rows = w_ref.shape[1]

    def put(dst, dst_off, src_off, n, slot=None):
        slot = n if slot is None else slot
        piece = w_ref[0, :, src_off:src_off + n].astype(_BF16)
        if slot > n:
            piece = jnp.concatenate([piece, jnp.zeros((rows, slot - n), _BF16)], axis=1)
        dst[0, :, dst_off:dst_off + slot] = piece

    kv = NSA_KV_HEADS * HEAD_DIM
    put(o1_ref, H1_Q_A, offs["q_a"], NSA_HEADS * HEAD_DIM)
    put(o1_ref, H1_KS, offs["k_s"], 4 * kv)
    put(o1_ref, H1_Q_B, offs["q_b"], DSA_HEADS * HEAD_DIM)
    for h in range(IDX_HEADS):
        put(o1_ref, H1_Q_IDX + h * LANES, offs["q_idx"] + h * IDX_DIM, IDX_DIM, LANES)
    put(o1_ref, H1_K_IDX, offs["k_idx"], IDX_DIM, H1_COLS - H1_K_IDX)

    put(o2_ref, H2_UV, offs["uv"], 2 * GMLP_WIDTH)
    put(o2_ref, H2_KC, offs["k_c"], 2 * kv)
    put(o2_ref, H2_CKV, offs["c_kv"], KV_LATENT)
    per_head = NSA_GROUP * 3
    for h in range(NSA_KV_HEADS):
        put(o2_ref, H2_GN + h * LANES, offs["g_nsa"] + h * per_head, per_head, LANES)
    put(o2_ref, H2_WIDX, offs["w_idx"], IDX_HEADS, LANES)

    put(o4_ref, 0, offs["g_merge"], o4_ref.shape[2])


def _pack_w_in(w_in):
    layers, d, cols = w_in.shape
    offs, total = _in_offsets(d)
    assert total == cols
    tr = 128
    return pl.pallas_call(
        functools.partial(_pack_kernel, offs=offs),
        grid=(layers, d // tr),
        in_specs=[pl.BlockSpec((1, tr, cols), lambda l, r: (l, r, 0))],
        out_specs=[pl.BlockSpec((1, tr, H1_COLS), lambda l, r: (l, r, 0)),
                   pl.BlockSpec((1, tr, H2_COLS), lambda l, r: (l, r, 0)),
                   pl.BlockSpec((1, tr, N_BRANCHES * d), lambda l, r: (l, r, 0))],
        out_shape=[jax.ShapeDtypeStruct((layers, d, H1_COLS), _BF16),
                   jax.ShapeDtypeStruct((layers, d, H2_COLS), _BF16),
                   jax.ShapeDtypeStruct((layers, d, N_BRANCHES * d), _BF16)],
        compiler_params=_cparams(2, 48),
        name="pack_w_in",
    )(w_in)


def kernel(x, w_in, cmp_w1, cmp_w2, cmp_pe, w_uk, w_uv, kv_norm_g, gmlp_ln_g, gmlp_ln_b, gmlp_w_s, gmlp_b_s, w_branch, w_out, ln1_g, ln1_b, router_group_w, router_group_b, router_expert_w, router_expert_b, expert_w_gate, expert_w_up, expert_w_down, ln2_g, ln2_b):
    b, seq, d = x.shape
    depth = w_in.shape[0]
    alpha = float((2 * depth) ** 0.25)
    n_tok = b * seq
    slopes = 2.0 ** (-8.0 * jnp.arange(1, NSA_HEADS + 1, dtype=_F32) / NSA_HEADS)

    w1, w2, w4 = _pack_w_in(w_in)
    pe = cmp_pe.reshape(depth * 2, CMP_BLOCK, HEAD_DIM)
    cw1 = cmp_w1.astype(_BF16).reshape(depth * 2, CMP_BLOCK * HEAD_DIM, HEAD_DIM)
    cw2 = cmp_w2.astype(_BF16).reshape(depth * 2, HEAD_DIM, HEAD_DIM)
    wuk_t = w_uk.transpose(0, 1, 3, 2).astype(_BF16).reshape(depth * DSA_HEADS, HEAD_DIM, KV_LATENT)
    wuv = w_uv.astype(_BF16).reshape(depth * DSA_HEADS, KV_LATENT, HEAD_DIM)
    kvg = kv_norm_g.reshape(depth, 1, KV_LATENT)
    g_ln_g = gmlp_ln_g.reshape(depth, 1, GMLP_WIDTH)
    g_ln_b = gmlp_ln_b.reshape(depth, 1, GMLP_WIDTH)
    g_ws = gmlp_w_s.reshape(depth * GMLP_GROUPS, GMLP_CHUNK, GMLP_CHUNK)
    g_bs_t = gmlp_b_s.transpose(0, 2, 1)
    wb = w_branch.astype(_BF16).reshape(depth * N_BRANCHES, w_branch.shape[2], d)
    wo = w_out.astype(_BF16)
    n_route = N_GROUPS + N_EXPERTS
    w_r = jnp.pad(jnp.concatenate([router_group_w, router_expert_w], axis=2),
                  ((0, 0), (0, 0), (0, LANES - n_route)))
    b_r = jnp.pad(jnp.concatenate([router_group_b, router_expert_b], axis=1),
                  ((0, 0), (0, LANES - n_route))).reshape(depth, 1, LANES)
    wg = expert_w_gate.astype(_BF16).reshape(depth * N_EXPERTS, d, D_EXPERT)
    wu = expert_w_up.astype(_BF16).reshape(depth * N_EXPERTS, d, D_EXPERT)
    wd = expert_w_down.astype(_BF16).reshape(depth * N_EXPERTS, D_EXPERT, d)
    l1g, l1b = ln1_g.reshape(depth, 1, d), ln1_b.reshape(depth, 1, d)
    l2g, l2b = ln2_g.reshape(depth, 1, d), ln2_b.reshape(depth, 1, d)

    xt = x.reshape(n_tok, d)
    for l in range(depth):
        h1 = _matmul(xt, w1, l, _BF16, 768).reshape(b, seq, H1_COLS)
        h2 = _matmul(xt, w2, l, _F32, 640).reshape(b, seq, H2_COLS)
        gates = _matmul(xt, w4, l, _BF16, 768, act="sigmoid")

        kcv = _compress(h2, pe, cw1, cw2, l, seq)
        ocg, selmask = _cmp_select(slopes, h1, kcv, h2, seq)
        o_a = _sel_win(slopes, h1, selmask, ocg, h2, seq)

        cn = _rms_norm(h2, kvg, l, seq)
        o_b = _dsa(slopes, h1, h2, cn, wuk_t, wuv, l, seq)

        o_c = _gmlp(h2, g_ln_g, g_ln_b, g_ws, g_bs_t, l, seq)

        gs = _merge(o_a.reshape(n_tok, -1), o_b.reshape(n_tok, -1), o_c.reshape(n_tok, -1), gates, wb, l)
        x1e = _out_router(gs, wo, xt, l1g, l1b, w_r, b_r, l, alpha)
        xt = _moe(x1e, wg, wu, wd, l2g, l2b, l, alpha)
    return xt.reshape(b, seq, d)
```
